```python
import jax, jax.numpy as jnp
from jax import lax
import numpy as np

D_MODEL = 1024
BATCH = 2
SEQ = 16384
DEPTH = 2
DEC_BATCH = 8
DEC_SEQ = 64
PAST_LEN = 4096

CHUNK = 64
GLA_HEADS = 4
GLA_DK = 64
GLA_DV = 128
GLA_GATE_RANK = 16
GLA_TAU = 16.0
GLA_W = GLA_HEADS * GLA_DV
GLA_KW = GLA_HEADS * GLA_DK
SB_HEADS = 4
SB_DH = 64
SB_W = SB_HEADS * SB_DH
SB_QBLOCK = 128
SB_KBLOCK = 128
CONV_CH = 256
CONV_W = 3
MIX_W = GLA_W + SB_W + CONV_CH
D_FF = 4 * D_MODEL
LN_EPS = 1e-5
DEEPNORM_ALPHA = (2 * DEPTH) ** 0.25
DEEPNORM_BETA = (8 * DEPTH) ** -0.25
SPLITS = (GLA_KW, GLA_KW, GLA_W, GLA_W, GLA_GATE_RANK, SB_W, SB_W, SB_W, CONV_CH, CONV_CH, CONV_CH)
N_IN = GLA_KW * 2 + GLA_W * 2 + GLA_GATE_RANK + SB_W * 3 + CONV_CH * 3

kernel_name = "hymba_gla_stickbreak_shortconv_deepnorm_step"


def layer_norm(x, g, b):
    xf = x.astype(jnp.float32)
    mu = xf.mean(-1, keepdims=True)
    var = jnp.square(xf - mu).mean(-1, keepdims=True)
    return ((xf - mu) * lax.rsqrt(var + LN_EPS) * g + b).astype(x.dtype)


def split_cols(z):
    outs, idx = [], 0
    for n in SPLITS:
        outs.append(z[..., idx:idx + n])
        idx += n
    return outs


def gla_chunk(S, q, k, v, lg):
    C = q.shape[2]
    b = jnp.cumsum(lg, axis=2)
    causal = jnp.tril(jnp.ones((C, C), dtype=bool))
    diff = b[:, :, :, None, :] - b[:, :, None, :, :]
    decay = jnp.exp(jnp.where(causal[None, None, :, :, None], diff, -jnp.inf))
    att = jnp.einsum('bhtk,bhsk,bhtsk->bhts', q, k, decay)
    o = (jnp.einsum('bhtk,bhkv->bhtv', q * jnp.exp(b), S)
         + jnp.einsum('bhts,bhsv->bhtv', att, v))
    b_last = b[:, :, -1:, :]
    S_new = (jnp.exp(b_last[:, :, 0, :])[..., None] * S
             + jnp.einsum('bhsk,bhsv->bhkv', k * jnp.exp(b_last - b), v))
    return S_new, o


def gla_seq(S0, q, k, v, lg):
    B, T = q.shape[0], q.shape[1]
    C = min(CHUNK, T)
    n = T // C

    def to_blocks(a):
        return a.reshape(B, n, C, a.shape[2], a.shape[3]).transpose(1, 0, 3, 2, 4)

    S_fin, o = lax.scan(lambda S, xs: gla_chunk(S, *xs), S0,
                        (to_blocks(q), to_blocks(k), to_blocks(v), to_blocks(lg)))
    o = o.transpose(1, 0, 3, 2, 4).reshape(B, T, GLA_HEADS, GLA_DV)
    return S_fin, o


def sb_block(q, q_pos, k, v):
    B, Qb, K = q.shape[0], q.shape[1], k.shape[1]
    nb = K // SB_KBLOCK
    k_pos = jnp.arange(K, dtype=jnp.int32)
    z = jnp.einsum('bqhd,bkhd->bhqk', q, k).astype(jnp.float32) * (SB_DH ** -0.5)
    mask = k_pos[None, :] < q_pos[:, None]
    lk = jnp.where(mask, jax.nn.log_sigmoid(-z), 0.0)
    lkb = lk.reshape(B, SB_HEADS, Qb, nb, SB_KBLOCK)
    later = (jnp.arange(SB_KBLOCK)[:, None] > jnp.arange(SB_KBLOCK)[None, :]).astype(jnp.float32)
    after_in = jnp.einsum('bhqnj,js->bhqns', lkb, later, precision=lax.Precision.HIGHEST)
    bsum = lkb.sum(-1)
    suff = bsum.sum(-1, keepdims=True) - jnp.cumsum(bsum, axis=-1)
    after = (after_in + suff[..., None]).reshape(B, SB_HEADS, Qb, K)
    A = jnp.exp(jnp.where(mask, jax.nn.log_sigmoid(z) + after, -jnp.inf))
    return jnp.einsum('bhqk,bkhd->bqhd', A.astype(v.dtype), v)


def sb_attention(q, k, v, past_len):
    B, T = q.shape[0], q.shape[1]
    Qb = min(SB_QBLOCK, T)
    n = T // Qb
    outs = []
    for i in range(n):
        q_pos = past_len + i * Qb + jnp.arange(Qb, dtype=jnp.int32)
        kend = past_len + (i + 1) * Qb
        kpad = -(-kend // SB_KBLOCK) * SB_KBLOCK
        kk, vv = k[:, :kend], v[:, :kend]
        if kpad > kend:
            padw = ((0, 0), (0, kpad - kend), (0, 0), (0, 0))
            kk, vv = jnp.pad(kk, padw), jnp.pad(vv, padw)
        outs.append(sb_block(q[:, i * Qb:(i + 1) * Qb], q_pos, kk, vv))
    return jnp.concatenate(outs, axis=1).reshape(B, T, SB_W)


def short_conv(u, prev, w):
    T = u.shape[1]
    up = jnp.concatenate([prev.astype(u.dtype), u], axis=1)
    y = w[CONV_W - 1] * up[:, CONV_W - 1:CONV_W - 1 + T]
    for i in range(CONV_W - 1):
        y = y + w[i] * up[:, i:i + T]
    return y, up[:, -(CONV_W - 1):]


def token_mixer(x, w_in, w_gate_up, b_gate, gla_norm_g, conv_w, w_out,
                past_k, past_v, S0, conv_prev, past_len):
    B, T, _ = x.shape
    f32 = jnp.float32
    gq, gk, gv, gr, glow, sq, sk, sv, cb, cc, ch = split_cols(x @ w_in)

    def heads(a, h):
        return a.reshape(B, T, h, -1)

    lg = jax.nn.log_sigmoid((glow @ w_gate_up + b_gate).astype(f32)) / GLA_TAU
    S_fin, go = gla_seq(S0.astype(f32),
                        heads(gq, GLA_HEADS).astype(f32) * (GLA_DK ** -0.5),
                        heads(gk, GLA_HEADS).astype(f32),
                        heads(gv, GLA_HEADS).astype(f32),
                        heads(lg, GLA_HEADS))
    go = go * lax.rsqrt(jnp.mean(go * go, axis=-1, keepdims=True) + LN_EPS)
    go = go.reshape(B, T, GLA_W).astype(x.dtype) * gla_norm_g * jax.nn.silu(gr)

    k_new, v_new = heads(sk, SB_HEADS), heads(sv, SB_HEADS)
    if past_k is None:
        k_all, v_all = k_new, v_new
    else:
        k_all = jnp.concatenate([past_k.astype(k_new.dtype), k_new], axis=1)
        v_all = jnp.concatenate([past_v.astype(v_new.dtype), v_new], axis=1)
    so = sb_attention(heads(sq, SB_HEADS), k_all, v_all, past_len)

    cy, conv_state = short_conv(cc * ch, conv_prev, conv_w)
    co = cb * cy

    out = jnp.concatenate([go, so, co], axis=-1) @ w_out
    return out, k_new, v_new, S_fin.astype(x.dtype), conv_state


def setup_inputs(seed: int = 0) -> dict:
    key = jax.random.key(seed)
    ks = jax.random.split(key, 24)

    def nrm(k, shape, s=1.0):
        return jax.random.normal(k, shape, jnp.float32) * s

    return {
        "x_prompt": nrm(ks[0], (BATCH, SEQ, D_MODEL)),
        "x_sample": nrm(ks[1], (DEC_BATCH, DEC_SEQ, D_MODEL)),
        "cache_sb_k": nrm(ks[2], (DEPTH, DEC_BATCH, PAST_LEN, SB_HEADS, SB_DH)),
        "cache_sb_v": nrm(ks[3], (DEPTH, DEC_BATCH, PAST_LEN, SB_HEADS, SB_DH)),
        "state_gla": nrm(ks[4], (DEPTH, DEC_BATCH, GLA_HEADS, GLA_DK, GLA_DV), 2.0),
        "state_conv": nrm(ks[5], (DEPTH, DEC_BATCH, CONV_W - 1, CONV_CH)),
        "ln_in_g": 1.0 + nrm(ks[6], (D_MODEL,), 0.02),
        "ln_in_b": nrm(ks[7], (D_MODEL,), 0.02),
        "w_in": nrm(ks[8], (DEPTH, D_MODEL, N_IN), D_MODEL ** -0.5),
        "w_gate_up": nrm(ks[9], (DEPTH, GLA_GATE_RANK, GLA_KW), GLA_GATE_RANK ** -0.5),
        "b_gate": nrm(ks[10], (DEPTH, GLA_KW), 0.02),
        "gla_norm_g": 1.0 + nrm(ks[11], (DEPTH, GLA_W), 0.02),
        "conv_w": nrm(ks[12], (DEPTH, CONV_W, CONV_CH), CONV_W ** -0.5),
        "w_out": nrm(ks[13], (DEPTH, MIX_W, D_MODEL), MIX_W ** -0.5 * DEEPNORM_BETA),
        "ln1_g": 1.0 + nrm(ks[14], (DEPTH, D_MODEL), 0.02),
        "ln1_b": nrm(ks[15], (DEPTH, D_MODEL), 0.02),
        "w_up": nrm(ks[16], (DEPTH, D_MODEL, D_FF), D_MODEL ** -0.5),
        "w_down": nrm(ks[17], (DEPTH, D_FF, D_MODEL), D_FF ** -0.5 * DEEPNORM_BETA),
        "ln2_g": 1.0 + nrm(ks[18], (DEPTH, D_MODEL), 0.02),
        "ln2_b": nrm(ks[19], (DEPTH, D_MODEL), 0.02),
    }


def reference(x_prompt, x_sample, cache_sb_k, cache_sb_v, state_gla, state_conv,
              ln_in_g, ln_in_b, w_in, w_gate_up, b_gate, gla_norm_g, conv_w, w_out,
              ln1_g, ln1_b, w_up, w_down, ln2_g, ln2_b):

    def trunk(x, c_k, c_v, s_gla, s_conv):
        B = x.shape[0]
        past_len = 0 if c_k is None else c_k.shape[2]
        x = layer_norm(x, ln_in_g, ln_in_b)
        ks_, vs_, gs_, cs_ = [], [], [], []
        for l in range(DEPTH):
            if c_k is None:
                pk, pv = None, None
                S0 = jnp.zeros((B, GLA_HEADS, GLA_DK, GLA_DV), jnp.float32)
                cprev = jnp.zeros((B, CONV_W - 1, CONV_CH), x.dtype)
            else:
                pk, pv, S0, cprev = c_k[l], c_v[l], s_gla[l], s_conv[l]
            mix, kn, vn, Sf, cst = token_mixer(x, w_in[l], w_gate_up[l], b_gate[l], gla_norm_g[l],
                                               conv_w[l], w_out[l], pk, pv, S0, cprev, past_len)
            x = layer_norm(DEEPNORM_ALPHA * x + mix, ln1_g[l], ln1_b[l])
            ff = jnp.square(jax.nn.relu(x @ w_up[l])) @ w_down[l]
            x = layer_norm(DEEPNORM_ALPHA * x + ff, ln2_g[l], ln2_b[l])
            ks_.append(kn); vs_.append(vn); gs_.append(Sf); cs_.append(cst)
        return x, jnp.stack(ks_), jnp.stack(vs_), jnp.stack(gs_), jnp.stack(cs_)

    y_prompt, k_p, v_p, gla_p, conv_p = trunk(x_prompt, None, None, None, None)
    y_sample, k_s, v_s, gla_s, conv_s = trunk(x_sample, cache_sb_k, cache_sb_v, state_gla, state_conv)
    return (y_prompt, y_sample, k_p, v_p, gla_p, conv_p, k_s, v_s, gla_s, conv_s)
```

```python
import functools

import jax
import jax.numpy as jnp
from jax import lax
from jax.experimental import pallas as pl
from jax.experimental.pallas import tpu as pltpu

F32 = jnp.float32
BF16 = jnp.bfloat16

D_MODEL = 1024
DEPTH = 2
CHUNK = 64
GLA_HEADS = 4
GLA_DK = 64
GLA_DV = 128
GLA_GATE_RANK = 16
GLA_TAU = 16.0
GLA_W = GLA_HEADS * GLA_DV
GLA_KW = GLA_HEADS * GLA_DK
SB_HEADS = 4
SB_DH = 64
SB_W = SB_HEADS * SB_DH
CONV_CH = 256
CONV_W = 3
MIX_W = GLA_W + SB_W + CONV_CH
D_FF = 4 * D_MODEL
LN_EPS = 1e-5
DEEPNORM_ALPHA = (2 * DEPTH) ** 0.25

LANES = 128
SUBLANES = 8
SB_KBLOCK = 128
SB_DEAD_LOG = -104.0
VMEM_LIMIT_BYTES = 56 * 1024 * 1024

_C_GQ = 0
_C_GK = _C_GQ + GLA_KW
_C_GV = _C_GK + GLA_KW
_C_GR = _C_GV + GLA_W
_C_SQ = _C_GR + GLA_W
_C_SK = _C_SQ + SB_W
_C_SV = _C_SK + SB_W
_C_CB = _C_SV + SB_W
_C_CC = _C_CB + CONV_CH
_C_CH = _C_CC + CONV_CH
_C_GLOW = _C_CH + CONV_CH
N_IN_PAD = _C_GLOW + LANES
_GLOW_SRC = 2 * GLA_KW + 2 * GLA_W


def _layer_norm(x, g, b):
    mu = jnp.mean(x, axis=-1, keepdims=True)
    xc = x - mu
    var = jnp.mean(xc * xc, axis=-1, keepdims=True)
    return xc * lax.rsqrt(var + LN_EPS) * g + b


def _softplus_neg_abs(z):
    return jnp.log1p(jnp.exp(-jnp.abs(z)))


def _split_bf16(x, n):
    parts = []
    r = x
    for _ in range(n - 1):
        p = r.astype(BF16)
        parts.append(p)
        r = r - p.astype(F32)
    parts.append(r.astype(BF16))
    return parts


def _in_proj_kernel(pre_ln, tm, x_ref, lng_ref, lnb_ref, w_ref, wg_ref, bg_ref, cw_ref, cprev_ref,
                    gq_ref, gk_ref, gv_ref, gr_ref, lg_ref, sq_ref, skf_ref, svf_ref, skb_ref, svb_ref,
                    co_ref, cst_ref, uext_ref):
    i = pl.program_id(1)
    x = x_ref[0]
    if pre_ln:
        x = _layer_norm(x, lng_ref[...], lnb_ref[...])
    xb = x.astype(BF16)

    def proj(a, b):
        return jnp.dot(xb, w_ref[:, a:b], preferred_element_type=F32)

    gq_ref[0] = proj(_C_GQ, _C_GK) * (GLA_DK ** -0.5)
    gk_ref[0] = proj(_C_GK, _C_GV)
    gv_ref[0] = proj(_C_GV, _C_GR).astype(BF16)
    gr_ref[0] = proj(_C_GR, _C_SQ)
    sq_ref[0] = (proj(_C_SQ, _C_SK) * (SB_DH ** -0.5)).astype(BF16)
    sk = proj(_C_SK, _C_SV)
    skf_ref[0] = sk
    skb_ref[0] = sk.astype(BF16)
    sv = proj(_C_SV, _C_CB)
    svf_ref[0] = sv
    svb_ref[0] = sv.astype(BF16)

    glow = proj(_C_GLOW, N_IN_PAD)
    u = jnp.dot(glow.astype(BF16), wg_ref[...], preferred_element_type=F32) + bg_ref[...]
    lg_ref[0] = (jnp.minimum(u, 0.0) - _softplus_neg_abs(u)) * (1.0 / GLA_TAU)

    cb = proj(_C_CB, _C_CC)
    uc = proj(_C_CC, _C_CH) * proj(_C_CH, _C_GLOW)

    @pl.when(i == 0)
    def _():
        uext_ref[SUBLANES - 2:SUBLANES, :] = cprev_ref[0]

    uext_ref[SUBLANES:SUBLANES + tm, :] = uc
    um1 = uext_ref[SUBLANES - 1:SUBLANES - 1 + tm, :]
    um2 = uext_ref[SUBLANES - 2:SUBLANES - 2 + tm, :]
    cw = cw_ref[...]
    cy = (cw[2:3, :] * uc + cw[0:1, :] * um2) + cw[1:2, :] * um1
    co_ref[0] = (cb * cy).astype(BF16)
    tail = uc[tm - 2:tm, :]
    uext_ref[SUBLANES - 2:SUBLANES, :] = tail

    @pl.when(i == pl.num_programs(1) - 1)
    def _():
        cst_ref[0] = tail


def _in_proj(x, pre_ln, ln_g, ln_b, w, wg, bg, cw, cprev):
    B, T, D = x.shape
    tm = min(512, T)
    n = T // tm
    row = lambda width: pl.BlockSpec((1, tm, width), lambda b, i: (b, i, 0))
    const2 = lambda a: pl.BlockSpec(a.shape, lambda b, i: (0, 0), pipeline_mode=pl.Buffered(1))
    out_shapes = [
        jax.ShapeDtypeStruct((B, T, GLA_KW), F32),
        jax.ShapeDtypeStruct((B, T, GLA_KW), F32),
        jax.ShapeDtypeStruct((B, T, GLA_W), BF16),
        jax.ShapeDtypeStruct((B, T, GLA_W), F32),
        jax.ShapeDtypeStruct((B, T, GLA_KW), F32),
        jax.ShapeDtypeStruct((B, T, SB_W), BF16),
        jax.ShapeDtypeStruct((B, T, SB_W), F32),
        jax.ShapeDtypeStruct((B, T, SB_W), F32),
        jax.ShapeDtypeStruct((B, T, SB_W), BF16),
        jax.ShapeDtypeStruct((B, T, SB_W), BF16),
        jax.ShapeDtypeStruct((B, T, CONV_CH), BF16),
        jax.ShapeDtypeStruct((B, CONV_W - 1, CONV_CH), F32),
    ]
    out_specs = [row(s.shape[-1]) for s in out_shapes[:-1]]
    out_specs.append(pl.BlockSpec((1, CONV_W - 1, CONV_CH), lambda b, i: (b, 0, 0)))
    return pl.pallas_call(
        functools.partial(_in_proj_kernel, pre_ln, tm),
        grid=(B, n),
        in_specs=[row(D), const2(ln_g), const2(ln_b), const2(w), const2(wg), const2(bg), const2(cw),
                  pl.BlockSpec((1, CONV_W - 1, CONV_CH), lambda b, i: (b, 0, 0))],
        out_specs=out_specs,
        out_shape=out_shapes,
        scratch_shapes=[pltpu.VMEM((SUBLANES + tm, CONV_CH), F32)],
        compiler_params=pltpu.CompilerParams(dimension_semantics=("arbitrary", "arbitrary"),
                                             vmem_limit_bytes=VMEM_LIMIT_BYTES),
        name="in_proj",
    )(x, ln_g, ln_b, w, wg, bg, cw, cprev)


def _gla_kernel(n_chunks, q_ref, k_ref, v_ref, lg_ref, gr_ref, gn_ref, s0_ref,
                go_ref, sfin_ref, st_ref, b_ref, vf_ref):
    i = pl.program_id(1)
    C = CHUNK

    @pl.when(i == 0)
    def _():
        for h in range(GLA_HEADS):
            st_ref[h] = s0_ref[0, h].T

    r = lax.broadcasted_iota(jnp.int32, (C, C), 0)
    c = lax.broadcasted_iota(jnp.int32, (C, C), 1)
    tril = jnp.where(c <= r, 1.0, 0.0).astype(BF16)
    kk = lax.shift_right_logical(lax.broadcasted_iota(jnp.int32, (GLA_KW, GLA_W), 0), 6)
    hh = lax.shift_right_logical(lax.broadcasted_iota(jnp.int32, (GLA_KW, GLA_W), 1), 7)
    head_sum = jnp.where(kk == hh, 1.0, 0.0).astype(BF16)
    rid8 = lax.broadcasted_iota(jnp.int32, (SUBLANES, GLA_KW), 0)
    nt = (((1,), (1,)), ((), ()))
    tn = (((0,), (0,)), ((), ()))

    def chunk(ci, carry):
        r0 = pl.multiple_of(ci * C, C)
        q = q_ref[0, pl.ds(r0, C), :]
        k = k_ref[0, pl.ds(r0, C), :]
        v = v_ref[0, pl.ds(r0, C), :]
        lg = lg_ref[0, pl.ds(r0, C), :]
        b = sum(jnp.dot(tril, p, preferred_element_type=F32) for p in _split_bf16(lg, 3))
        b_ref[...] = b
        vf_ref[...] = v.astype(F32)
        b_last = b[C - 1:C, :]
        qe = (q * jnp.exp(b)).astype(BF16)
        kd = (k * jnp.exp(b_last - b)).astype(BF16)
        e_last = jnp.exp(b_last)

        o_inter = []
        for h in range(GLA_HEADS):
            ks = slice(h * GLA_DK, (h + 1) * GLA_DK)
            vs = slice(h * GLA_DV, (h + 1) * GLA_DV)
            st = st_ref[h]
            o_inter.append(lax.dot_general(qe[:, ks], st.astype(BF16), nt, preferred_element_type=F32))
            st_ref[h] = st * e_last[:, ks] + lax.dot_general(v[:, vs], kd[:, ks], tn,
                                                           preferred_element_type=F32)

        ob = [None] * (C // SUBLANES)
        for g in range(C // SUBLANES):
            t0 = g * SUBLANES
            rows = C - t0
            bq = b[t0:, :]
            qq = q[t0:, :]
            parts = []
            for j in range(SUBLANES):
                s = t0 + j
                diff = bq - b_ref[pl.ds(s, 1), :]
                if j > 0:
                    head = jnp.where(rid8 >= j, diff[:SUBLANES], -jnp.inf)
                    diff = head if rows == SUBLANES else jnp.concatenate([head, diff[SUBLANES:]], axis=0)
                parts.append(qq * jnp.exp(diff) * k_ref[0, pl.ds(r0 + s, 1), :])
            p_all = jnp.concatenate(parts, axis=0).astype(BF16)
            att = jnp.dot(p_all, head_sum, preferred_element_type=F32)
            for j in range(SUBLANES):
                vrow = vf_ref[pl.ds(t0 + j, 1), :]
                for bi in range(g, C // SUBLANES):
                    lo = j * rows + (bi - g) * SUBLANES
                    blk = att[lo:lo + SUBLANES, :] * vrow
                    ob[bi] = blk if ob[bi] is None else ob[bi] + blk
        o = jnp.concatenate(ob, axis=0) + jnp.concatenate(o_inter, axis=1)

        normed = []
        for h in range(GLA_HEADS):
            oh = o[:, h * GLA_DV:(h + 1) * GLA_DV]
            normed.append(oh * lax.rsqrt(jnp.mean(oh * oh, axis=-1, keepdims=True) + LN_EPS))
        gr = gr_ref[0, pl.ds(r0, C), :]
        out = jnp.concatenate(normed, axis=1) * gn_ref[...] * (gr * jax.nn.sigmoid(gr))
        go_ref[0, pl.ds(r0, C), :] = out.astype(BF16)
        return carry

    lax.fori_loop(0, n_chunks, chunk, 0)

    @pl.when(i == pl.num_programs(1) - 1)
    def _():
        for h in range(GLA_HEADS):
            sfin_ref[0, h] = st_ref[h].T


def _gla(gq, gk, gv, lg, gr, gn, s0):
    B, T, _ = gq.shape
    tg = min(512, T)
    n = T // tg
    row = lambda width: pl.BlockSpec((1, tg, width), lambda b, i: (b, i, 0))
    state = pl.BlockSpec((1, GLA_HEADS, GLA_DK, GLA_DV), lambda b, i: (b, 0, 0, 0))
    return pl.pallas_call(
        functools.partial(_gla_kernel, tg // CHUNK),
        grid=(B, n),
        in_specs=[row(GLA_KW), row(GLA_KW), row(GLA_W), row(GLA_KW), row(GLA_W),
                  pl.BlockSpec(gn.shape, lambda b, i: (0, 0)), state],
        out_specs=[row(GLA_W), state],
        out_shape=[jax.ShapeDtypeStruct((B, T, GLA_W), BF16),
                   jax.ShapeDtypeStruct((B, GLA_HEADS, GLA_DK, GLA_DV), F32)],
        scratch_shapes=[pltpu.VMEM((GLA_HEADS, GLA_DV, GLA_DK), F32),
                        pltpu.VMEM((CHUNK, GLA_KW), F32),
                        pltpu.VMEM((CHUNK, GLA_W), F32)],
        compiler_params=pltpu.CompilerParams(dimension_semantics=("arbitrary", "arbitrary"),
                                             vmem_limit_bytes=VMEM_LIMIT_BYTES),
        name="gla",
    )(gq, gk, gv, lg, gr, gn, s0)


def _suffix_weights(tk):
    j = lax.broadcasted_iota(jnp.int32, (tk, tk + LANES), 0)
    s = lax.broadcasted_iota(jnp.int32, (tk, tk + LANES), 1)
    return jnp.where(jnp.logical_or(j > s, s >= tk), 1.0, 0.0).astype(BF16)


def _sb_kernel(tq, n_past, past_is_self, q_ref, kd_ref, vd_ref, kp_ref, vp_ref, o_ref):
    kp_ref = kp_ref.at[0, 0]
    vp_ref = vp_ref.at[0, 0]
    i = pl.program_id(1)
    nt = (((1,), (1,)), ((), ()))
    if past_is_self:
        base = i * tq
        nblk = lax.shift_right_logical(base, 7)
    else:
        base = n_past
        nblk = n_past // SB_KBLOCK
    w_diag = _suffix_weights(tq)
    w_past = w_diag if tq == SB_KBLOCK else _suffix_weights(SB_KBLOCK)
    qb = q_ref[0]

    def block(qh, kh, vh, w, tk, prefix, mask):
        z = lax.dot_general(qh, kh, nt, preferred_element_type=F32)
        sp = _softplus_neg_abs(z)
        log_beta = jnp.minimum(z, 0.0) - sp
        log_keep = jnp.minimum(-z, 0.0) - sp
        if mask is not None:
            log_keep = jnp.where(mask, log_keep, 0.0)
        hi, lo = _split_bf16(log_keep, 2)
        res = jnp.dot(hi, w, preferred_element_type=F32) + jnp.dot(lo, w, preferred_element_type=F32)
        after = res[:, :tk]
        if prefix is not None:
            after = after + prefix
        a = jnp.exp(log_beta + after)
        if mask is not None:
            a = jnp.where(mask, a, 0.0)
        return jnp.dot(a.astype(BF16), vh, preferred_element_type=F32), res[:, tk:]

    r = lax.broadcasted_iota(jnp.int32, (tq, tq), 0)
    c = lax.broadcasted_iota(jnp.int32, (tq, tq), 1)
    causal = c < r
    kd = kd_ref[0]
    vd = vd_ref[0]
    outs, prefs = [], []
    for h in range(SB_HEADS):
        hs = slice(h * SB_DH, (h + 1) * SB_DH)
        o_h, p_h = block(qb[:, hs], kd[:, hs], vd[:, hs], w_diag, tq, None, causal)
        outs.append(o_h)
        prefs.append(p_h)

    def live(prefs):
        m = functools.reduce(jnp.maximum, prefs)
        return jnp.max(m) > SB_DEAD_LOG

    def cond(carry):
        j, alive = carry[0], carry[1]
        return jnp.logical_and(j < nblk, alive)

    def body(carry):
        j = carry[0]
        outs = list(carry[2:2 + SB_HEADS])
        prefs = list(carry[2 + SB_HEADS:])
        start = pl.multiple_of(base - (j + 1) * SB_KBLOCK, SB_KBLOCK)
        kb = kp_ref[pl.ds(start, SB_KBLOCK), :].astype(BF16)
        vb = vp_ref[pl.ds(start, SB_KBLOCK), :].astype(BF16)
        for h in range(SB_HEADS):
            hs = slice(h * SB_DH, (h + 1) * SB_DH)
            o_h, p_h = block(qb[:, hs], kb[:, hs], vb[:, hs], w_past, SB_KBLOCK, prefs[h], None)
            outs[h] = outs[h] + o_h
            prefs[h] = prefs[h] + p_h
        return (j + 1, live(prefs), *outs, *prefs)

    final = lax.while_loop(cond, body, (jnp.int32(0), live(prefs), *outs, *prefs))
    o_ref[0] = jnp.concatenate(final[2:2 + SB_HEADS], axis=1).astype(BF16)


def _sb(sq, k_new, v_new, k_past, v_past, layer, past_is_self):
    B, T, _ = sq.shape
    P = k_past.shape[2]
    tq = min(SB_KBLOCK, T)
    n = T // tq
    assert T % tq == 0 and P % SB_KBLOCK == 0 and (n == 1 or tq == SB_KBLOCK)
    row = pl.BlockSpec((1, tq, SB_W), lambda b, i: (b, i, 0))
    full =pl.BlockSpec((1, 1, P, SB_W), lambda b, i: (layer, b, 0, 0))
    return pl.pallas_call(
        functools.partial(_sb_kernel, tq, P, past_is_self),
        grid=(B, n),
        in_specs=[row, row, row, full, full],
        out_specs=row,
        out_shape=jax.ShapeDtypeStruct((B, T, SB_W), BF16),
        compiler_params=pltpu.CompilerParams(dimension_semantics=("arbitrary", "arbitrary"),
                                             vmem_limit_bytes=VMEM_LIMIT_BYTES),
        name="sb",
    )(sq, k_new, v_new, k_past, v_past)


FF_CHUNK = 1024


def _post_kernel(pre_ln, x_ref, go_ref, so_ref, co_ref, lng_ref, lnb_ref, wo_ref, g1_ref, b1_ref,
                 wu_ref, wd_ref, g2_ref, b2_ref, y_ref):
    x = x_ref[...]
    if pre_ln:
        x = _layer_norm(x, lng_ref[...], lnb_ref[...])
    mix = (jnp.dot(go_ref[...], wo_ref[0:GLA_W, :], preferred_element_type=F32)
           + jnp.dot(so_ref[...], wo_ref[GLA_W:GLA_W + SB_W, :], preferred_element_type=F32)
           + jnp.dot(co_ref[...], wo_ref[GLA_W + SB_W:MIX_W, :], preferred_element_type=F32))
    x1 = _layer_norm(DEEPNORM_ALPHA * x + mix, g1_ref[...], b1_ref[...])
    x1b = x1.astype(BF16)
    ff = None
    for c0 in range(0, D_FF, FF_CHUNK):
        h = jnp.dot(x1b, wu_ref[:, c0:c0 + FF_CHUNK], preferred_element_type=F32)
        h = jnp.square(jnp.maximum(h, 0.0)).astype(BF16)
        part = jnp.dot(h, wd_ref[c0:c0 + FF_CHUNK, :], preferred_element_type=F32)
        ff = part if ff is None else ff + part
    y_ref[...] = _layer_norm(DEEPNORM_ALPHA * x1 + ff, g2_ref[...], b2_ref[...])


def _post(x, go, so, co, pre_ln, ln_g, ln_b, wo, g1, b1, wu, wd, g2, b2):
    N, D = x.shape
    tm = min(512, N)
    row = lambda width: pl.BlockSpec((tm, width), lambda i: (i, 0))
    const = lambda a: pl.BlockSpec(a.shape, lambda i: (0, 0), pipeline_mode=pl.Buffered(1))
    return pl.pallas_call(
        functools.partial(_post_kernel, pre_ln),
        grid=(N // tm,),
        in_specs=[row(D), row(GLA_W), row(SB_W), row(CONV_CH), const(ln_g), const(ln_b), const(wo),
                  const(g1), const(b1), const(wu), const(wd), const(g2), const(b2)],
        out_specs=row(D),
        out_shape=jax.ShapeDtypeStruct((N, D), F32),
        compiler_params=pltpu.CompilerParams(dimension_semantics=("arbitrary",),
                                             vmem_limit_bytes=VMEM_LIMIT_BYTES),
        name="post",
    )(x, go, so, co, ln_g, ln_b, wo, g1, b1, wu, wd, g2, b2)


def _prep_layer(l, w_in, w_gate_up, b_gate, gla_norm_g, conv_w, w_out, ln1_g, ln1_b, w_up, w_down,
                ln2_g, ln2_b):
    wl = w_in[l]
    w = jnp.concatenate([wl[:, :_GLOW_SRC], wl[:, _GLOW_SRC + GLA_GATE_RANK:],
                         wl[:, _GLOW_SRC:_GLOW_SRC + GLA_GATE_RANK],
                         jnp.zeros((D_MODEL, LANES - GLA_GATE_RANK), F32)], axis=1).astype(BF16)
    wg = jnp.concatenate([w_gate_up[l], jnp.zeros((LANES - GLA_GATE_RANK, GLA_KW), F32)], axis=0).astype(BF16)
    r2 = lambda a: a.reshape(1, -1)
    return dict(w=w, wg=wg, bg=r2(b_gate[l]), gn=r2(gla_norm_g[l]), cw=conv_w[l],
                wo=w_out[l].astype(BF16), g1=r2(ln1_g[l]), b1=r2(ln1_b[l]),
                wu=w_up[l].astype(BF16), wd=w_down[l].astype(BF16), g2=r2(ln2_g[l]), b2=r2(ln2_b[l]))


def _trunk(x, c_k, c_v, s_gla, s_conv, ln_g, ln_b, layers):
    B, T, D = x.shape
    ks, vs, gs, cs = [], [], [], []
    for l, p in enumerate(layers):
        pre_ln = l == 0
        if c_k is None:
            s0 = jnp.zeros((B, GLA_HEADS, GLA_DK, GLA_DV), F32)
            cprev = jnp.zeros((B, CONV_W - 1, CONV_CH), F32)
        else:
            s0, cprev = s_gla[l], s_conv[l]
        gq, gk, gv, gr, lg, sq, skf, svf, skb, svb, co, cst = _in_proj(
            x, pre_ln, ln_g, ln_b, p["w"], p["wg"], p["bg"], p["cw"], cprev)
        go, s_fin = _gla(gq, gk, gv, lg, gr, p["gn"], s0)
        if c_k is None:
            so = _sb(sq, skb, svb, skb[None], svb[None], 0, True)
        else:
            so = _sb(sq, skb, svb, c_k.reshape(DEPTH, B, -1, SB_W), c_v.reshape(DEPTH, B, -1, SB_W), l, False)
        x = _post(x.reshape(B * T, D), go.reshape(B * T, GLA_W), so.reshape(B * T, SB_W),
                  co.reshape(B * T, CONV_CH), pre_ln, ln_g, ln_b, p["wo"], p["g1"], p["b1"],
                  p["wu"], p["wd"], p["g2"], p["b2"]).reshape(B, T, D)
        ks.append(skf.reshape(B, T, SB_HEADS, SB_DH))
        vs.append(svf.reshape(B, T, SB_HEADS, SB_DH))
        gs.append(s_fin)
        cs.append(cst)
    return x, jnp.stack(ks), jnp.stack(vs), jnp.stack(gs), jnp.stack(cs)


def kernel(x_prompt, x_sample, cache_sb_k, cache_sb_v, state_gla, state_conv, ln_in_g, ln_in_b, w_in,
           w_gate_up, b_gate, gla_norm_g, conv_w, w_out, ln1_g, ln1_b, w_up, w_down, ln2_g, ln2_b):
    layers = [_prep_layer(l, w_in, w_gate_up, b_gate, gla_norm_g, conv_w, w_out, ln1_g, ln1_b, w_up,
                          w_down, ln2_g, ln2_b) for l in range(DEPTH)]
    ln_g, ln_b = ln_in_g.reshape(1, -1), ln_in_b.reshape(1, -1)
    y_p, k_p, v_p, gla_p, conv_p = _trunk(x_prompt, None, None, None, None, ln_g, ln_b, layers)
    y_s, k_s, v_s, gla_s, conv_s = _trunk(x_sample, cache_sb_k, cache_sb_v, state_gla, state_conv,
                                          ln_g, ln_b, layers)
    return (y_p, y_s, k_p, v_p, gla_p, conv_p, k_s, v_s, gla_s, conv_s)
```

```python
import functools

import numpy as np

import jax
import jax.numpy as jnp
from jax import lax
from jax.experimental import pallas as pl
from jax.experimental.pallas import tpu as pltpu

F32 = jnp.float32
BF16 = jnp.bfloat16

D_MODEL = 1024
DEPTH = 2
CHUNK = 64
GLA_HEADS = 4
GLA_DK = 64
GLA_DV = 128
GLA_GATE_RANK = 16
GLA_TAU = 16.0
GLA_W = GLA_HEADS * GLA_DV
GLA_KW = GLA_HEADS * GLA_DK
SB_HEADS = 4
SB_DH = 64
SB_W = SB_HEADS * SB_DH
CONV_CH = 256
CONV_W = 3
MIX_W = GLA_W + SB_W + CONV_CH
D_FF = 4 * D_MODEL
LN_EPS = 1e-5
DEEPNORM_ALPHA = (2 * DEPTH) ** 0.25

LANES = 128
SUBLANES = 8
SB_KBLOCK = 128
SB_DEAD_LOG = -104.0
VMEM_LIMIT_BYTES = 56 * 1024 * 1024

_C_GQ = 0
_C_GK = _C_GQ + GLA_KW
_C_GV = _C_GK + GLA_KW
_C_GR = _C_GV + GLA_W
_C_SQ = _C_GR + GLA_W
_C_SK = _C_SQ + SB_W
_C_SV = _C_SK + SB_W
_C_CB = _C_SV + SB_W
_C_CC = _C_CB + CONV_CH
_C_CH = _C_CC + CONV_CH
_C_GLOW = _C_CH + CONV_CH
N_IN_PAD = _C_GLOW + LANES
_GLOW_SRC = 2 * GLA_KW + 2 * GLA_W


def _layer_norm(x, g, b):
    mu = jnp.mean(x, axis=-1, keepdims=True)
    xc = x - mu
    var = jnp.mean(xc * xc, axis=-1, keepdims=True)
    return xc * lax.rsqrt(var + LN_EPS) * g + b


def _softplus_neg_abs(z):
    return jnp.log1p(jnp.exp(-jnp.abs(z)))


def _split_bf16(x, n):
    parts = []
    r = x
    for _ in range(n - 1):
        p = r.astype(BF16)
        parts.append(p)
        r = r - p.astype(F32)
    parts.append(r.astype(BF16))
    return parts


def _in_proj_kernel(pre_ln, tm, x_ref, lng_ref, lnb_ref, w_ref, wg_ref, bg_ref, cw_ref, cprev_ref,
                    gq_ref, gk_ref, gv_ref, gr_ref, lg_ref, sq_ref, skf_ref, svf_ref, skb_ref, svb_ref,
                    co_ref, cst_ref, uext_ref):
    i = pl.program_id(1)
    x = x_ref[0]
    if pre_ln:
        x = _layer_norm(x, lng_ref[...], lnb_ref[...])
    xb = x.astype(BF16)

    def proj(a, b):
        return jnp.dot(xb, w_ref[:, a:b], preferred_element_type=F32)

    gq_ref[0] = proj(_C_GQ, _C_GK) * (GLA_DK ** -0.5)
    gk_ref[0] = proj(_C_GK, _C_GV)
    gv_ref[0] = proj(_C_GV, _C_GR).astype(BF16)
    gr_ref[0] = proj(_C_GR, _C_SQ)
    sq_ref[0] = (proj(_C_SQ, _C_SK) * (SB_DH ** -0.5)).astype(BF16)
    sk = proj(_C_SK, _C_SV)
    skf_ref[0] = sk
    skb_ref[0] = sk.astype(BF16)
    sv = proj(_C_SV, _C_CB)
    svf_ref[0] = sv
    svb_ref[0] = sv.astype(BF16)

    glow = proj(_C_GLOW, N_IN_PAD)
    u = jnp.dot(glow.astype(BF16), wg_ref[...], preferred_element_type=F32) + bg_ref[...]
    lg_ref[0] = (jnp.minimum(u, 0.0) - _softplus_neg_abs(u)) * (1.0 / GLA_TAU)

    cb = proj(_C_CB, _C_CC)
    uc = proj(_C_CC, _C_CH) * proj(_C_CH, _C_GLOW)

    @pl.when(i == 0)
    def _():
        uext_ref[SUBLANES - 2:SUBLANES, :] = cprev_ref[0]

    uext_ref[SUBLANES:SUBLANES + tm, :] = uc
    um1 = uext_ref[SUBLANES - 1:SUBLANES - 1 + tm, :]
    um2 = uext_ref[SUBLANES - 2:SUBLANES - 2 + tm, :]
    cw = cw_ref[...]
    cy = (cw[2:3, :] * uc + cw[0:1, :] * um2) + cw[1:2, :] * um1
    co_ref[0] = (cb * cy).astype(BF16)
    tail = uc[tm - 2:tm, :]
    uext_ref[SUBLANES - 2:SUBLANES, :] = tail

    @pl.when(i == pl.num_programs(1) - 1)
    def _():
        cst_ref[0] = tail


def _in_proj(x, pre_ln, ln_g, ln_b, w, wg, bg, cw, cprev):
    B, T, D = x.shape
    tm = min(512, T)
    n = T // tm
    row = lambda width: pl.BlockSpec((1, tm, width), lambda b, i: (b, i, 0))
    const2 = lambda a: pl.BlockSpec(a.shape, lambda b, i: (0, 0), pipeline_mode=pl.Buffered(1))
    out_shapes = [
        jax.ShapeDtypeStruct((B, T, GLA_KW), F32),
        jax.ShapeDtypeStruct((B, T, GLA_KW), F32),
        jax.ShapeDtypeStruct((B, T, GLA_W), BF16),
        jax.ShapeDtypeStruct((B, T, GLA_W), F32),
        jax.ShapeDtypeStruct((B, T, GLA_KW), F32),
        jax.ShapeDtypeStruct((B, T, SB_W), BF16),
        jax.ShapeDtypeStruct((B, T, SB_W), F32),
        jax.ShapeDtypeStruct((B, T, SB_W), F32),
        jax.ShapeDtypeStruct((B, T, SB_W), BF16),
        jax.ShapeDtypeStruct((B, T, SB_W), BF16),
        jax.ShapeDtypeStruct((B, T, CONV_CH), BF16),
        jax.ShapeDtypeStruct((B, CONV_W - 1, CONV_CH), F32),
    ]
    out_specs = [row(s.shape[-1]) for s in out_shapes[:-1]]
    out_specs.append(pl.BlockSpec((1, CONV_W - 1, CONV_CH), lambda b, i: (b, 0, 0)))
    return pl.pallas_call(
        functools.partial(_in_proj_kernel, pre_ln, tm),
        grid=(B, n),
        in_specs=[row(D), const2(ln_g), const2(ln_b), const2(w), const2(wg), const2(bg), const2(cw),
                  pl.BlockSpec((1, CONV_W - 1, CONV_CH), lambda b, i: (b, 0, 0))],
        out_specs=out_specs,
        out_shape=out_shapes,
        scratch_shapes=[pltpu.VMEM((SUBLANES + tm, CONV_CH), F32)],
        compiler_params=pltpu.CompilerParams(dimension_semantics=("arbitrary", "arbitrary"),
                                             vmem_limit_bytes=VMEM_LIMIT_BYTES),
        name="in_proj",
    )(x, ln_g, ln_b, w, wg, bg, cw, cprev)


GLA_LEVELS = (32, 16, 8, 4, 2, 1)
ATT_W = GLA_HEADS * CHUNK
B_PAD = SUBLANES
(_K_NEG_UP, _K_NEG_LO, _K_PAIR) = (0, len(GLA_LEVELS), 2 * len(GLA_LEVELS))
_K_EYE = 3 * len(GLA_LEVELS)
_K_POS4 = _K_EYE + 1
_K_ODD = _K_POS4 + 4
_K_COUNT = _K_ODD + 1


def _gla_constants():
    assert ATT_W == GLA_KW
    t = np.arange(CHUNK)[:, None]
    s = (np.arange(ATT_W) % CHUNK)[None, :]
    ninf = np.float32(-np.inf)
    out = np.zeros((_K_COUNT, CHUNK, ATT_W), np.float32)
    for li, h in enumerate(GLA_LEVELS):
        upper = np.broadcast_to((t & h) != 0, (CHUNK, ATT_W))
        out[_K_NEG_UP + li] = np.where(upper, 0.0, ninf)
        out[_K_NEG_LO + li] = np.where(upper, ninf, 0.0)
        out[_K_PAIR + li] = (((t ^ s) // (2 * h)) == 0) & upper & ((s & h) == 0)
    out[_K_EYE] = t == s
    for p in range(4):
        out[_K_POS4 + p] = np.broadcast_to((t & 3) == p, (CHUNK, ATT_W))
    out[_K_ODD] = np.broadcast_to((t & 1) == 1, (CHUNK, ATT_W))
    tril = (np.arange(CHUNK)[None, :] <= np.arange(CHUNK)[:, None]).astype(np.float32)
    hk = (np.arange(ATT_W)[:, None] // CHUNK == np.arange(GLA_KW)[None, :] // GLA_DK).astype(np.float32)
    hv = (np.arange(ATT_W)[:, None] // CHUNK == np.arange(GLA_W)[None, :] // GLA_DV).astype(np.float32)
    return (jnp.asarray(out), jnp.asarray(tril, BF16), jnp.asarray(hk, BF16), jnp.asarray(hv, BF16))


def _gla_kernel(n_chunks, q_ref, k_ref, v_ref, lg_ref, gr_ref, gn_ref, s0_ref, cst_ref, tril_ref, hk_ref,
                hv_ref, go_ref, sfin_ref, st_ref, b_ref):
    i = pl.program_id(1)
    C = CHUNK
    nt = (((1,), (1,)), ((), ()))
    tn = (((0,), (0,)), ((), ()))

    @pl.when(i == 0)
    def _():
        for h in range(GLA_HEADS):
            st_ref[h] = s0_ref[0, h].T
        b_ref[...] = jnp.zeros(b_ref.shape, F32)

    def reference_rows(h, b):
        if h >= 4:
            pieces = [jnp.broadcast_to(b_ref[pl.ds(B_PAD + m * 2 * h + h - 1, 1), :], (2 * h, GLA_KW))
                      for m in range(C // (2 * h))]
            return pieces[0] if len(pieces) == 1 else jnp.concatenate(pieces, axis=0)
        bm1 = b_ref[pl.ds(B_PAD - 1, C), :]
        if h == 1:
            return b + cst_ref[_K_ODD] * (bm1 - b)
        bm2 = b_ref[pl.ds(B_PAD - 2, C), :]
        bp1 = b_ref[pl.ds(B_PAD + 1, C), :]
        return ((cst_ref[_K_POS4 + 0] * bp1 + cst_ref[_K_POS4 + 1] * b)
                + (cst_ref[_K_POS4 + 2] * bm1 + cst_ref[_K_POS4 + 3] * bm2))

    def chunk(ci, carry):
        r0 = pl.multiple_of(ci * C, C)
        q = q_ref[0, pl.ds(r0, C), :]
        k = k_ref[0, pl.ds(r0, C), :]
        v = v_ref[0, pl.ds(r0, C), :]
        lg = lg_ref[0, pl.ds(r0, C), :]
        tril = tril_ref[...]
        b = sum(jnp.dot(tril, p, preferred_element_type=F32) for p in _split_bf16(lg, 3))
        b_ref[pl.ds(B_PAD, C), :] = b
        b_last = b[C - 1:C, :]
        qe = (q * jnp.exp(b)).astype(BF16)
        kd = (k * jnp.exp(b_last - b)).astype(BF16)
        e_last = jnp.exp(b_last)

        o_inter = []
        for h in range(GLA_HEADS):
            ks = slice(h * GLA_DK, (h + 1) * GLA_DK)
            vs = slice(h * GLA_DV, (h + 1) * GLA_DV)
            st = st_ref[h]
            o_inter.append(lax.dot_general(qe[:, ks], st.astype(BF16), nt, preferred_element_type=F32))
            st_ref[h] = st * e_last[:, ks] + lax.dot_general(v[:, vs], kd[:, ks], tn,
                                                           preferred_element_type=F32)

        hk = hk_ref[...]
        att = jnp.dot((q * k).astype(BF16), hk, preferred_element_type=F32) * cst_ref[_K_EYE]
        for li, h in enumerate(GLA_LEVELS):
            r = reference_rows(h, b)
            qt = (q * jnp.exp((b - r) + cst_ref[_K_NEG_UP + li])).astype(BF16)
            kt = (k * jnp.exp((r - b) + cst_ref[_K_NEG_LO + li])).astype(BF16)
            kbd = jnp.concatenate([kt] * GLA_HEADS, axis=0) * hk
            att = att + lax.dot_general(qt, kbd, nt, preferred_element_type=F32) * cst_ref[_K_PAIR + li]
        vbd = jnp.concatenate([v] * GLA_HEADS, axis=0) * hv_ref[...]
        o = jnp.dot(att.astype(BF16), vbd, preferred_element_type=F32) + jnp.concatenate(o_inter, axis=1)

        normed = []
        for h in range(GLA_HEADS):
            oh = o[:, h * GLA_DV:(h + 1) * GLA_DV]
            normed.append(oh * lax.rsqrt(jnp.mean(oh * oh, axis=-1, keepdims=True) + LN_EPS))
        gr = gr_ref[0, pl.ds(r0, C), :]
        out = jnp.concatenate(normed, axis=1) * gn_ref[...] * (gr * jax.nn.sigmoid(gr))
        go_ref[0, pl.ds(r0, C), :] = out.astype(BF16)
        return carry

    lax.fori_loop(0, n_chunks, chunk, 0, unroll=2)

    @pl.when(i == pl.num_programs(1) - 1)
    def _():
        for h in range(GLA_HEADS):
            sfin_ref[0, h] = st_ref[h].T


def _gla(gq, gk, gv, lg, gr, gn, s0):
    B, T, _ = gq.shape
    tg = min(512, T)
    n = T // tg
    row = lambda width: pl.BlockSpec((1, tg, width), lambda b, i: (b, i, 0))
    state = pl.BlockSpec((1, GLA_HEADS, GLA_DK, GLA_DV), lambda b, i: (b, 0, 0, 0))
    const = lambda a: pl.BlockSpec(a.shape, lambda b, i: (0,) * a.ndim, pipeline_mode=pl.Buffered(1))
    consts = _gla_constants()
    return pl.pallas_call(
        functools.partial(_gla_kernel, tg // CHUNK),
        grid=(B, n),
        in_specs=[row(GLA_KW), row(GLA_KW), row(GLA_W), row(GLA_KW), row(GLA_W), const(gn), state]
        + [const(c) for c in consts],
        out_specs=[row(GLA_W), state],
        out_shape=[jax.ShapeDtypeStruct((B, T, GLA_W), BF16),
                   jax.ShapeDtypeStruct((B, GLA_HEADS, GLA_DK, GLA_DV), F32)],
        scratch_shapes=[pltpu.VMEM((GLA_HEADS, GLA_DV, GLA_DK), F32),
                        pltpu.VMEM((CHUNK + 2 * B_PAD, GLA_KW), F32)],
        compiler_params=pltpu.CompilerParams(dimension_semantics=("arbitrary", "arbitrary"),
                                             vmem_limit_bytes=VMEM_LIMIT_BYTES),
        name="gla",
    )(gq, gk, gv, lg, gr, gn, s0, *consts)


SB_WINDOW_PAST = 256


def _suffix_weights(tk):
    j = np.arange(tk)[:, None]
    s = np.arange(tk + LANES)[None, :]
    return jnp.asarray((j > s) | (s >= tk), BF16)


def _sb_step(qb, kw, vw, w, prefix, q0_minus_k0):
    tq, tw = qb.shape[0], kw.shape[0]
    nt = (((1,), (1,)), ((), ()))
    heads = [slice(h * SB_DH, (h + 1) * SB_DH) for h in range(SB_HEADS)]
    z = jnp.concatenate([lax.dot_general(qb[:, hs], kw[:, hs], nt, preferred_element_type=F32)
                         for hs in heads], axis=0)
    sp = _softplus_neg_abs(z)
    log_beta = jnp.minimum(z, 0.0) - sp
    log_keep = jnp.minimum(-z, 0.0) - sp
    mask = None
    if q0_minus_k0 is not None:
        row = jnp.bitwise_and(lax.broadcasted_iota(jnp.int32, z.shape, 0), tq - 1)
        col = lax.broadcasted_iota(jnp.int32, z.shape, 1)
        mask = (col - row) < q0_minus_k0
        log_keep = jnp.where(mask, log_keep, 0.0)
    hi, lo = _split_bf16(log_keep, 2)
    res = jnp.dot(hi, w, preferred_element_type=F32) + jnp.dot(lo, w, preferred_element_type=F32)
    after = res[:, :tw]
    if prefix is not None:
        after = after + (prefix if tw == LANES else jnp.concatenate([prefix] * (tw // LANES), axis=1))
    a = jnp.exp(log_beta + after)
    if mask is not None:
        a = jnp.where(mask, a, 0.0)
    a = a.astype(BF16)
    outs = [jnp.dot(a[h * tq:(h + 1) * tq], vw[:, hs], preferred_element_type=F32)
            for h, hs in enumerate(heads)]
    return outs, res[:, tw:]


def _sb_kernel(tq, tw, wp, n_past, past_is_self, *refs):
    if past_is_self:
        q_ref, kp_ref, vp_ref, ww_ref, wb_ref, o_ref = refs
    else:
        q_ref, kd_ref, vd_ref, kp_ref, vp_ref, ww_ref, wb_ref, o_ref = refs
    kp_ref = kp_ref.at[0, 0]
    vp_ref = vp_ref.at[0, 0]
    i = pl.program_id(1)
    qb = q_ref[0]

    if past_is_self:
        rest = pl.multiple_of(jnp.maximum(i * tq - wp, 0), SB_KBLOCK)
        kw = kp_ref[pl.ds(rest, tw), :]
        vw = vp_ref[pl.ds(rest, tw), :]
        q0_minus_k0 = i * tq - rest
    else:
        rest = n_past - wp
        pad = jnp.zeros((tw - wp - tq, SB_W), BF16)
        kw = jnp.concatenate([kp_ref[rest:n_past, :].astype(BF16), kd_ref[0], pad], axis=0)
        vw = jnp.concatenate([vp_ref[rest:n_past, :].astype(BF16), vd_ref[0], pad], axis=0)
        q0_minus_k0 = wp
    outs, prefs = _sb_step(qb, kw, vw, ww_ref[...], None, q0_minus_k0)
    nblk = rest // SB_KBLOCK if isinstance(rest, int) else lax.shift_right_logical(rest, 7)

    def cond(carry):
        j, prefs = carry[0], carry[1]
        return jnp.logical_and(j < nblk, jnp.max(prefs) > SB_DEAD_LOG)

    def body(carry):
        j, prefs = carry[0], carry[1]
        start = pl.multiple_of(rest - (j + 1) * SB_KBLOCK, SB_KBLOCK)
        kb = kp_ref[pl.ds(start, SB_KBLOCK), :].astype(BF16)
        vb = vp_ref[pl.ds(start, SB_KBLOCK), :].astype(BF16)
        o_new, p_new = _sb_step(qb, kb, vb, wb_ref[...], prefs, None)
        return (j + 1, prefs + p_new, *[o + n for o, n in zip(carry[2:], o_new)])

    final = lax.while_loop(cond, body, (jnp.int32(0), prefs, *outs))
    o_ref[0] = jnp.concatenate(final[2:], axis=1).astype(BF16)


def _sb(sq, k_new, v_new, k_past, v_past, layer, past_is_self):
    B, T, _ = sq.shape
    P = k_past.shape[2]
    tq = min(SB_KBLOCK, T)
    n = T // tq
    wp = min(SB_WINDOW_PAST, P - tq if past_is_self else P)
    tw = -(-(wp + tq) // LANES) * LANES
    assert T % tq == 0 and tq & (tq - 1) == 0 and P % SB_KBLOCK == 0 and wp % SB_KBLOCK == 0
    assert n == 1 or tq == SB_KBLOCK
    assert not past_is_self or tw == wp + tq
    row = pl.BlockSpec((1, tq, SB_W), lambda b, i: (b, i, 0))
    full = pl.BlockSpec((1, 1, P, SB_W), lambda b, i: (layer, b, 0, 0), pipeline_mode=pl.Buffered(1))
    const = lambda a: pl.BlockSpec(a.shape, lambda b, i: (0, 0), pipeline_mode=pl.Buffered(1))
    ww, wb = _suffix_weights(tw), _suffix_weights(SB_KBLOCK)
    args = [sq] + ([] if past_is_self else [k_new, v_new]) + [k_past, v_past, ww, wb]
    specs = [row] + ([] if past_is_self else [row, row]) + [full, full, const(ww), const(wb)]
    return pl.pallas_call(
        functools.partial(_sb_kernel, tq, tw, wp, P, past_is_self),
        grid=(B, n),
        in_specs=specs,
        out_specs=row,
        out_shape=jax.ShapeDtypeStruct((B, T, SB_W), BF16),
        compiler_params=pltpu.CompilerParams(dimension_semantics=("arbitrary", "arbitrary"),
                                             vmem_limit_bytes=VMEM_LIMIT_BYTES),
        name="sb",
    )(*args)


FF_CHUNK = 1024


def _post_kernel(pre_ln, x_ref, go_ref, so_ref, co_ref, lng_ref, lnb_ref, wo_ref, g1_ref, b1_ref,
                 wu_ref, wd_ref, g2_ref, b2_ref, y_ref):
    x = x_ref[...]
    if pre_ln:
        x = _layer_norm(x, lng_ref[...], lnb_ref[...])
    mix = (jnp.dot(go_ref[...], wo_ref[0:GLA_W, :], preferred_element_type=F32)
           + jnp.dot(so_ref[...], wo_ref[GLA_W:GLA_W + SB_W, :], preferred_element_type=F32)
           + jnp.dot(co_ref[...], wo_ref[GLA_W + SB_W:MIX_W, :], preferred_element_type=F32))
    x1 = _layer_norm(DEEPNORM_ALPHA * x + mix, g1_ref[...], b1_ref[...])
    x1b = x1.astype(BF16)
    ff = None
    for c0 in range(0, D_FF, FF_CHUNK):
        h = jnp.dot(x1b, wu_ref[:, c0:c0 + FF_CHUNK], preferred_element_type=F32)
        h = jnp.square(jnp.maximum(h, 0.0)).astype(BF16)
        part = jnp.dot(h, wd_ref[c0:c0 + FF_CHUNK, :], preferred_element_type=F32)
        ff = part if ff is None else ff + part
    y_ref[...] = _layer_norm(DEEPNORM_ALPHA * x1 + ff, g2_ref[...], b2_ref[...])


def _post(x, go, so, co, pre_ln, ln_g, ln_b, wo, g1, b1, wu, wd, g2, b2):
    N, D = x.shape
    tm = min(512, N)
    row = lambda width: pl.BlockSpec((tm, width), lambda i: (i, 0))
    const = lambda a: pl.BlockSpec(a.shape, lambda i: (0, 0), pipeline_mode=pl.Buffered(1))
    return pl.pallas_call(
        functools.partial(_post_kernel, pre_ln),
        grid=(N // tm,),
        in_specs=[row(D), row(GLA_W), row(SB_W), row(CONV_CH), const(ln_g), const(ln_b), const(wo),
                  const(g1), const(b1), const(wu), const(wd), const(g2), const(b2)],
        out_specs=row(D),
        out_shape=jax.ShapeDtypeStruct((N, D), F32),
        compiler_params=pltpu.CompilerParams(dimension_semantics=("arbitrary",),
                                             vmem_limit_bytes=VMEM_LIMIT_BYTES),
        name="post",
    )(x, go, so, co, ln_g, ln_b, wo, g1, b1, wu, wd, g2, b2)


def _prep_layer(l, w_in, w_gate_up, b_gate, gla_norm_g, conv_w, w_out, ln1_g, ln1_b, w_up, w_down,
                ln2_g, ln2_b):
    wl = w_in[l]
    w = jnp.concatenate([wl[:, :_GLOW_SRC], wl[:, _GLOW_SRC + GLA_GATE_RANK:],
                         wl[:, _GLOW_SRC:_GLOW_SRC + GLA_GATE_RANK],
                         jnp.zeros((D_MODEL, LANES - GLA_GATE_RANK), F32)], axis=1).astype(BF16)
    wg = jnp.concatenate([w_gate_up[l], jnp.zeros((LANES - GLA_GATE_RANK, GLA_KW), F32)], axis=0).astype(BF16)
    r2 = lambda a: a.reshape(1, -1)
    return dict(w=w, wg=wg, bg=r2(b_gate[l]), gn=r2(gla_norm_g[l]), cw=conv_w[l],
                wo=w_out[l].astype(BF16), g1=r2(ln1_g[l]), b1=r2(ln1_b[l]),
                wu=w_up[l].astype(BF16), wd=w_down[l].astype(BF16), g2=r2(ln2_g[l]), b2=r2(ln2_b[l]))


def _trunk(x, c_k, c_v, s_gla, s_conv, ln_g, ln_b, layers):
    B, T, D = x.shape
    ks, vs, gs, cs = [], [], [], []
    for l, p in enumerate(layers):
        pre_ln = l == 0
        if c_k is None:
            s0 = jnp.zeros((B, GLA_HEADS, GLA_DK, GLA_DV), F32)
            cprev = jnp.zeros((B, CONV_W - 1, CONV_CH), F32)
        else:
            s0, cprev = s_gla[l], s_conv[l]
        gq, gk, gv, gr, lg, sq, skf, svf, skb, svb, co, cst = _in_proj(
            x, pre_ln, ln_g, ln_b, p["w"], p["wg"], p["bg"], p["cw"], cprev)
        go, s_fin = _gla(gq, gk, gv, lg, gr, p["gn"], s0)
        if c_k is None:
            so = _sb(sq, skb, svb, skb[None], svb[None], 0, True)
        else:
            so = _sb(sq, skb, svb, c_k.reshape(DEPTH, B, -1, SB_W), c_v.reshape(DEPTH, B, -1, SB_W), l, False)
        x = _post(x.reshape(B * T, D), go.reshape(B * T, GLA_W), so.reshape(B * T, SB_W),
                  co.reshape(B * T, CONV_CH), pre_ln, ln_g, ln_b, p["wo"], p["g1"], p["b1"],
                  p["wu"], p["wd"], p["g2"], p["b2"]).reshape(B, T, D)
        ks.append(skf.reshape(B, T, SB_HEADS, SB_DH))
        vs.append(svf.reshape(B, T, SB_HEADS, SB_DH))
        gs.append(s_fin)
        cs.append(cst)
    return x, jnp.stack(ks), jnp.stack(vs), jnp.stack(gs), jnp.stack(cs)


def kernel(x_prompt, x_sample, cache_sb_k, cache_sb_v, state_gla, state_conv, ln_in_g, ln_in_b, w_in,
           w_gate_up, b_gate, gla_norm_g, conv_w, w_out, ln1_g, ln1_b, w_up, w_down, ln2_g, ln2_b):
    layers = [_prep_layer(l, w_in, w_gate_up, b_gate, gla_norm_g, conv_w, w_out, ln1_g, ln1_b, w_up,
                          w_down, ln2_g, ln2_b) for l in range(DEPTH)]
    ln_g, ln_b = ln_in_g.reshape(1, -1), ln_in_b.reshape(1, -1)
    y_p, k_p, v_p, gla_p, conv_p = _trunk(x_prompt, None, None, None, None, ln_g, ln_b, layers)
    y_s, k_s, v_s, gla_s, conv_s = _trunk(x_sample, cache_sb_k, cache_sb_v, state_gla, state_conv,
                                          ln_g, ln_b, layers)
    return (y_p, y_s, k_p, v_p, gla_p, conv_p, k_s, v_s, gla_s, conv_s)
```

```python
import functools

import numpy as np

import jax
import jax.numpy as jnp
from jax import lax
from jax.experimental import pallas as pl
from jax.experimental.pallas import tpu as pltpu

F32 = jnp.float32
BF16 = jnp.bfloat16

D_MODEL = 1024
DEPTH = 2
CHUNK = 64
GLA_HEADS = 4
GLA_DK = 64
GLA_DV = 128
GLA_GATE_RANK = 16
GLA_TAU = 16.0
GLA_W = GLA_HEADS * GLA_DV
GLA_KW = GLA_HEADS * GLA_DK
SB_HEADS = 4
SB_DH = 64
SB_W = SB_HEADS * SB_DH
CONV_CH = 256
CONV_W = 3
MIX_W = GLA_W + SB_W + CONV_CH
D_FF = 4 * D_MODEL
LN_EPS = 1e-5
DEEPNORM_ALPHA = (2 * DEPTH) ** 0.25

LANES = 128
SUBLANES = 8
SB_KBLOCK = 128
SB_DEAD_LOG = -104.0
VMEM_LIMIT_BYTES = 56 * 1024 * 1024

_C_GQ = 0
_C_GK = _C_GQ + GLA_KW
_C_GV = _C_GK + GLA_KW
_C_GR = _C_GV + GLA_W
_C_SQ = _C_GR + GLA_W
_C_SK = _C_SQ + SB_W
_C_SV = _C_SK + SB_W
_C_CB = _C_SV + SB_W
_C_CC = _C_CB + CONV_CH
_C_CH = _C_CC + CONV_CH
_C_GLOW = _C_CH + CONV_CH
N_IN_PAD = _C_GLOW + LANES
_GLOW_SRC = 2 * GLA_KW + 2 * GLA_W


def _layer_norm(x, g, b):
    mu = jnp.mean(x, axis=-1, keepdims=True)
    xc = x - mu
    var = jnp.mean(xc * xc, axis=-1, keepdims=True)
    return xc * lax.rsqrt(var + LN_EPS) * g + b


def _softplus_neg_abs(z):
    return jnp.log1p(jnp.exp(-jnp.abs(z)))


def _split_bf16(x, n):
    parts = []
    r = x
    for _ in range(n - 1):
        p = r.astype(BF16)
        parts.append(p)
        r = r - p.astype(F32)
    parts.append(r.astype(BF16))
    return parts


def _in_proj_kernel(pre_ln, tm, x_ref, lng_ref, lnb_ref, w_ref, wg_ref, bg_ref, cw_ref, cprev_ref,
                    gq_ref, gk_ref, gv_ref, gr_ref, lg_ref, sq_ref, skf_ref, svf_ref, skb_ref, svb_ref,
                    co_ref, cst_ref, uext_ref):
    i = pl.program_id(1)
    x = x_ref[0]
    if pre_ln:
        x = _layer_norm(x, lng_ref[...], lnb_ref[...])
    xb = x.astype(BF16)

    def proj(a, b):
        return jnp.dot(xb, w_ref[:, a:b], preferred_element_type=F32)

    gq_ref[0] = proj(_C_GQ, _C_GK) * (GLA_DK ** -0.5)
    gk_ref[0] = proj(_C_GK, _C_GV)
    gv_ref[0] = proj(_C_GV, _C_GR).astype(BF16)
    gr_ref[0] = proj(_C_GR, _C_SQ)
    sq_ref[0] = (proj(_C_SQ, _C_SK) * (SB_DH ** -0.5)).astype(BF16)
    sk = proj(_C_SK, _C_SV)
    skf_ref[0] = sk.T
    skb_ref[0] = sk.astype(BF16)
    sv = proj(_C_SV, _C_CB)
    svf_ref[0] = sv.T
    svb_ref[0] = sv.astype(BF16)

    glow = proj(_C_GLOW, N_IN_PAD)
    u = jnp.dot(glow.astype(BF16), wg_ref[...], preferred_element_type=F32) + bg_ref[...]
    lg_ref[0] = (jnp.minimum(u, 0.0) - _softplus_neg_abs(u)) * (1.0 / GLA_TAU)

    cb = proj(_C_CB, _C_CC)
    uc = proj(_C_CC, _C_CH) * proj(_C_CH, _C_GLOW)

    @pl.when(i == 0)
    def _():
        uext_ref[SUBLANES - 2:SUBLANES, :] = cprev_ref[0]

    uext_ref[SUBLANES:SUBLANES + tm, :] = uc
    um1 = uext_ref[SUBLANES - 1:SUBLANES - 1 + tm, :]
    um2 = uext_ref[SUBLANES - 2:SUBLANES - 2 + tm, :]
    cw = cw_ref[...]
    cy = (cw[2:3, :] * uc + cw[0:1, :] * um2) + cw[1:2, :] * um1
    co_ref[0] = (cb * cy).astype(BF16)
    tail = uc[tm - 2:tm, :]
    uext_ref[SUBLANES - 2:SUBLANES, :] = tail

    @pl.when(i == pl.num_programs(1) - 1)
    def _():
        cst_ref[0] = tail


def _in_proj(x, pre_ln, ln_g, ln_b, w, wg, bg, cw, cprev):
    B, T, D = x.shape
    tm = min(512, T)
    n = T // tm
    row = lambda width: pl.BlockSpec((1, tm, width), lambda b, i: (b, i, 0))
    const2 = lambda a: pl.BlockSpec(a.shape, lambda b, i: (0, 0), pipeline_mode=pl.Buffered(1))
    out_shapes = [
        jax.ShapeDtypeStruct((B, T, GLA_KW), F32),
        jax.ShapeDtypeStruct((B, T, GLA_KW), F32),
        jax.ShapeDtypeStruct((B, T, GLA_W), BF16),
        jax.ShapeDtypeStruct((B, T, GLA_W), F32),
        jax.ShapeDtypeStruct((B, T, GLA_KW), F32),
        jax.ShapeDtypeStruct((B, T, SB_W), BF16),
        jax.ShapeDtypeStruct((B, SB_W, T), F32),
        jax.ShapeDtypeStruct((B, SB_W, T), F32),
        jax.ShapeDtypeStruct((B, T, SB_W), BF16),
        jax.ShapeDtypeStruct((B, T, SB_W), BF16),
        jax.ShapeDtypeStruct((B, T, CONV_CH), BF16),
        jax.ShapeDtypeStruct((B, CONV_W - 1, CONV_CH), F32),
    ]
    out_specs = [row(s.shape[-1]) for s in out_shapes[:-1]]
    out_specs[6] = out_specs[7] = pl.BlockSpec((1, SB_W, tm), lambda b, i: (b, 0, i))
    out_specs.append(pl.BlockSpec((1, CONV_W - 1, CONV_CH), lambda b, i: (b, 0, 0)))
    return pl.pallas_call(
        functools.partial(_in_proj_kernel, pre_ln, tm),
        grid=(B, n),
        in_specs=[row(D), const2(ln_g), const2(ln_b), const2(w), const2(wg), const2(bg), const2(cw),
                  pl.BlockSpec((1, CONV_W - 1, CONV_CH), lambda b, i: (b, 0, 0))],
        out_specs=out_specs,
        out_shape=out_shapes,
        scratch_shapes=[pltpu.VMEM((SUBLANES + tm, CONV_CH), F32)],
        compiler_params=pltpu.CompilerParams(dimension_semantics=("arbitrary", "arbitrary"),
                                             vmem_limit_bytes=VMEM_LIMIT_BYTES),
        name="in_proj",
    )(x, ln_g, ln_b, w, wg, bg, cw, cprev)


GLA_LEVELS = (32, 16, 8, 4, 2, 1)
ATT_W = GLA_HEADS * CHUNK
B_PAD = SUBLANES
(_K_NEG_UP, _K_NEG_LO, _K_PAIR) = (0, len(GLA_LEVELS), 2 * len(GLA_LEVELS))
_K_EYE = 3 * len(GLA_LEVELS)
_K_POS4 = _K_EYE + 1
_K_ODD = _K_POS4 + 4
_K_COUNT = _K_ODD + 1


def _gla_constants():
    assert ATT_W == GLA_KW
    t = np.arange(CHUNK)[:, None]
    s = (np.arange(ATT_W) % CHUNK)[None, :]
    ninf = np.float32(-np.inf)
    out = np.zeros((_K_COUNT, CHUNK, ATT_W), np.float32)
    for li, h in enumerate(GLA_LEVELS):
        upper = np.broadcast_to((t & h) != 0, (CHUNK, ATT_W))
        out[_K_NEG_UP + li] = np.where(upper, 0.0, ninf)
        out[_K_NEG_LO + li] = np.where(upper, ninf, 0.0)
        out[_K_PAIR + li] = (((t ^ s) // (2 * h)) == 0) & upper & ((s & h) == 0)
    out[_K_EYE] = t == s
    for p in range(4):
        out[_K_POS4 + p] = np.broadcast_to((t & 3) == p, (CHUNK, ATT_W))
    out[_K_ODD] = np.broadcast_to((t & 1) == 1, (CHUNK, ATT_W))
    tril = (np.arange(CHUNK)[None, :] <= np.arange(CHUNK)[:, None]).astype(np.float32)
    hk = (np.arange(ATT_W)[:, None] // CHUNK == np.arange(GLA_KW)[None, :] // GLA_DK).astype(np.float32)
    hv = (np.arange(ATT_W)[:, None] // CHUNK == np.arange(GLA_W)[None, :] // GLA_DV).astype(np.float32)
    return (jnp.asarray(out), jnp.asarray(tril, BF16), jnp.asarray(hk, BF16), jnp.asarray(hv, BF16))


def _gla_kernel(nc, q_ref, k_ref, v_ref, lg_ref, gr_ref, gn_ref, s0_ref, cst_ref, tril_ref, hk_ref,
                hv_ref, go_ref, sfin_ref, st_ref, b_ref):
    i = pl.program_id(1)
    C = CHUNK
    T = nc * C
    nt = (((1,), (1,)), ((), ()))
    tn = (((0,), (0,)), ((), ()))
    bnn = (((2,), (1,)), ((0,), (0,)))
    bnt = (((2,), (2,)), ((0,), (0,)))

    @pl.when(i == 0)
    def _():
        for h in range(GLA_HEADS):
            st_ref[h] = s0_ref[0, h].T
        b_ref[...] = jnp.zeros(b_ref.shape, F32)

    def chunks(a):
        return a.reshape(nc, C, a.shape[-1])

    q2, k2 = q_ref[0], k_ref[0]
    q, k, v = chunks(q2), chunks(k2), chunks(v_ref[0])
    tril = jnp.broadcast_to(tril_ref[...], (nc, C, C))
    b = sum(lax.dot_general(tril, chunks(p), bnn, preferred_element_type=F32)
            for p in _split_bf16(lg_ref[0], 3))
    b_ref[pl.ds(B_PAD, T), :] = b.reshape(T, GLA_KW)
    b_last = b[:, C - 1:C, :]
    qe = (q * jnp.exp(b)).astype(BF16)
    kd = (k * jnp.exp(b_last - b)).astype(BF16)
    e_last = jnp.exp(b_last)

    def reference_rows(h):
        if h >= 4:
            pieces = [jnp.broadcast_to(b[:, m * 2 * h + h - 1:m * 2 * h + h, :], (nc, 2 * h, GLA_KW))
                      for m in range(C // (2 * h))]
            return pieces[0] if len(pieces) == 1 else jnp.concatenate(pieces, axis=1)
        bm1 = chunks(b_ref[pl.ds(B_PAD - 1, T), :])
        if h == 1:
            return b + cst_ref[_K_ODD] * (bm1 - b)
        bm2 = chunks(b_ref[pl.ds(B_PAD - 2, T), :])
        bp1 = chunks(b_ref[pl.ds(B_PAD + 1, T), :])
        return ((cst_ref[_K_POS4 + 0] * bp1 + cst_ref[_K_POS4 + 1] * b)
                + (cst_ref[_K_POS4 + 2] * bm1 + cst_ref[_K_POS4 + 3] * bm2))

    hk = hk_ref[...]
    att = chunks(jnp.dot((q2 * k2).astype(BF16), hk, preferred_element_type=F32)) * cst_ref[_K_EYE]
    for li, h in enumerate(GLA_LEVELS):
        r = reference_rows(h)
        qt = (q * jnp.exp((b - r) + cst_ref[_K_NEG_UP + li])).astype(BF16)
        kt = (k * jnp.exp((r - b) + cst_ref[_K_NEG_LO + li])).astype(BF16)
        kbd = jnp.concatenate([kt] * GLA_HEADS, axis=1) * hk
        att = att + lax.dot_general(qt, kbd, bnt, preferred_element_type=F32) * cst_ref[_K_PAIR + li]
    vbd = jnp.concatenate([v] * GLA_HEADS, axis=1) * hv_ref[...]
    o_intra = lax.dot_general(att.astype(BF16), vbd, bnn, preferred_element_type=F32)

    st = [st_ref[h] for h in range(GLA_HEADS)]
    o_inter = []
    for c in range(nc):
        per_head = []
        for h in range(GLA_HEADS):
            ks = slice(h * GLA_DK, (h + 1) * GLA_DK)
            vs = slice(h * GLA_DV, (h + 1) * GLA_DV)
            per_head.append(lax.dot_general(qe[c][:, ks], st[h].astype(BF16), nt,
                                            preferred_element_type=F32))
            st[h] = st[h] * e_last[c][:, ks] + lax.dot_general(v[c][:, vs], kd[c][:, ks], tn,
                                                               preferred_element_type=F32)
        o_inter.append(jnp.concatenate(per_head, axis=1))
    for h in range(GLA_HEADS):
        st_ref[h] = st[h]
    o = o_intra.reshape(T, GLA_W) + jnp.concatenate(o_inter, axis=0)

    normed = []
    for h in range(GLA_HEADS):
        oh = o[:, h * GLA_DV:(h + 1) * GLA_DV]
        normed.append(oh * lax.rsqrt(jnp.mean(oh * oh, axis=-1, keepdims=True) + LN_EPS))
    gr = gr_ref[0]
    out = jnp.concatenate(normed, axis=1) * gn_ref[...] * (gr * jax.nn.sigmoid(gr))
    go_ref[0] = out.astype(BF16)

    @pl.when(i == pl.num_programs(1) - 1)
    def _():
        for h in range(GLA_HEADS):
            sfin_ref[0, h] = st_ref[h].T


def _gla(gq, gk, gv, lg, gr, gn, s0):
    B, T, _ = gq.shape
    tg = min(512, T)
    n = T // tg
    row = lambda width: pl.BlockSpec((1, tg, width), lambda b, i: (b, i, 0))
    state = pl.BlockSpec((1, GLA_HEADS, GLA_DK, GLA_DV), lambda b, i: (b, 0, 0, 0))
    const = lambda a: pl.BlockSpec(a.shape, lambda b, i: (0,) * a.ndim, pipeline_mode=pl.Buffered(1))
    consts = _gla_constants()
    return pl.pallas_call(
        functools.partial(_gla_kernel, tg // CHUNK),
        grid=(B, n),
        in_specs=[row(GLA_KW), row(GLA_KW), row(GLA_W), row(GLA_KW), row(GLA_W), const(gn), state]
        + [const(c) for c in consts],
        out_specs=[row(GLA_W), state],
        out_shape=[jax.ShapeDtypeStruct((B, T, GLA_W), BF16),
                   jax.ShapeDtypeStruct((B, GLA_HEADS, GLA_DK, GLA_DV), F32)],
        scratch_shapes=[pltpu.VMEM((GLA_HEADS, GLA_DV, GLA_DK), F32),
                        pltpu.VMEM((tg + 2 * B_PAD, GLA_KW), F32)],
        compiler_params=pltpu.CompilerParams(dimension_semantics=("arbitrary", "arbitrary"),
                                             vmem_limit_bytes=VMEM_LIMIT_BYTES),
        name="gla",
    )(gq, gk, gv, lg, gr, gn, s0, *consts)


SB_WINDOW_PAST = 256


def _suffix_weights(tk):
    j = np.arange(tk)[:, None]
    s = np.arange(tk + LANES)[None, :]
    return jnp.asarray((j > s) | (s >= tk), BF16)


SB_MASKED_LOGIT = -1e30


def _sb_step(q_tiles, kws, vws, w, prefix, lims, col_minus_row):
    tq, tw = q_tiles[0].shape[0], kws[0].shape[0]
    nt = (((1,), (1,)), ((), ()))
    heads = [slice(h * SB_DH, (h + 1) * SB_DH) for h in range(SB_HEADS)]
    zs = []
    for u, (qb, kw) in enumerate(zip(q_tiles, kws)):
        z = jnp.concatenate([lax.dot_general(qb[:, hs], kw[:, hs], nt, preferred_element_type=F32)
                             for hs in heads], axis=0)
        if lims is not None:
            z = jnp.where(col_minus_row < lims[u], z, SB_MASKED_LOGIT)
        zs.append(z)
    z = zs[0] if len(zs) == 1 else jnp.concatenate(zs, axis=0)
    log_beta = jnp.minimum(z, 0.0) - jnp.log(1.0 + jnp.exp(-jnp.abs(z)))
    log_keep = log_beta - z
    hi, lo = _split_bf16(log_keep, 2)
    res = jnp.dot(hi, w, preferred_element_type=F32) + jnp.dot(lo, w, preferred_element_type=F32)
    after = res[:, :tw]
    if prefix is not None:
        after = after + (prefix if tw == LANES else jnp.concatenate([prefix] * (tw // LANES), axis=1))
    a = jnp.exp(log_beta + after).astype(BF16)
    outs = [[jnp.dot(a[(u * SB_HEADS + h) * tq:(u * SB_HEADS + h + 1) * tq], vw[:, hs],
                     preferred_element_type=F32) for h, hs in enumerate(heads)]
            for u, vw in enumerate(vws)]
    return outs, res[:, tw:]


def _sb_kernel(tq, nsub, tw, wp, n_past, layer, past_is_self, *refs):
    if past_is_self:
        q_ref, kp_ref, vp_ref, ww_ref, wb_ref, cmr_ref, o_ref = refs
        kp_ref = kp_ref.at[0, 0]
        vp_ref = vp_ref.at[0, 0]
    else:
        (q_ref, kd_ref, vd_ref, kwin_ref, vwin_ref, kc_hbm, vc_hbm, ww_ref, wb_ref, cmr_ref, o_ref,
         kbuf_ref, vbuf_ref, sem) = refs
    i = pl.program_id(1)
    q_tiles = [q_ref[0, u * tq:(u + 1) * tq, :] for u in range(nsub)]

    if past_is_self:
        q0s = [(i * nsub + u) * tq for u in range(nsub)]
        rests = [pl.multiple_of(jnp.maximum(q0 - wp, 0), SB_KBLOCK) for q0 in q0s]
        kws = [kp_ref[pl.ds(r, tw), :] for r in rests]
        vws = [vp_ref[pl.ds(r, tw), :] for r in rests]
        lims = [q0 - r for q0, r in zip(q0s, rests)]
    else:
        rests = [n_past - wp]
        pad = jnp.zeros((tw - wp - tq, SB_W), BF16)
        kws = [jnp.concatenate([kwin_ref[0, 0].T.astype(BF16), kd_ref[0], pad], axis=0)]
        vws = [jnp.concatenate([vwin_ref[0, 0].T.astype(BF16), vd_ref[0], pad], axis=0)]
        lims = [wp]

    def past_block(start):
        if past_is_self:
            return kp_ref[pl.ds(start, SB_KBLOCK), :], vp_ref[pl.ds(start, SB_KBLOCK), :]
        b = pl.program_id(0)
        copies = [pltpu.make_async_copy(src.at[layer, b, :, pl.ds(start, SB_KBLOCK)], dst, sem.at[n])
                  for n, (src, dst) in enumerate(((kc_hbm, kbuf_ref), (vc_hbm, vbuf_ref)))]
        for cp in copies:
            cp.start()
        for cp in copies:
            cp.wait()
        return kbuf_ref[...].T.astype(BF16), vbuf_ref[...].T.astype(BF16)
    outs, prefs = _sb_step(q_tiles, kws, vws, ww_ref[...], None, lims, cmr_ref[...])

    rows = SB_HEADS * tq
    for u in range(nsub):
        rest = rests[u]
        nblk = rest // SB_KBLOCK if isinstance(rest, int) else lax.shift_right_logical(rest, 7)

        def cond(carry, nblk=nblk):
            j, pref = carry[0], carry[1]
            return jnp.logical_and(j < nblk, jnp.max(pref) > SB_DEAD_LOG)

        def body(carry, rest=rest, u=u):
            j, pref = carry[0], carry[1]
            start = pl.multiple_of(rest - (j + 1) * SB_KBLOCK, SB_KBLOCK)
            kb, vb = past_block(start)
            o_new, p_new = _sb_step([q_tiles[u]], [kb], [vb], wb_ref[...], pref, None, None)
            return (j + 1, pref + p_new, *[o + n for o, n in zip(carry[2:], o_new[0])])

        final = lax.while_loop(cond, body, (jnp.int32(0), prefs[u * rows:(u + 1) * rows], *outs[u]))
        o_ref[0, u * tq:(u + 1) * tq, :] = jnp.concatenate(final[2:], axis=1).astype(BF16)


def _sb(sq, k_new, v_new, k_past, v_past, layer, past_is_self):
    B, T, _ = sq.shape
    P = k_past.shape[2] if past_is_self else k_past.shape[3]
    tq = min(SB_KBLOCK, T)
    nsub = 2 if past_is_self and T % (2 * tq) == 0 else 1
    n = T // (nsub * tq)
    wp = min(SB_WINDOW_PAST, P - tq if past_is_self else P)
    tw = -(-(wp + tq) // LANES) * LANES
    assert T % tq == 0 and tq & (tq - 1) == 0 and P % SB_KBLOCK == 0 and wp % SB_KBLOCK == 0
    assert T == tq or tq == SB_KBLOCK
    assert not past_is_self or tw == wp + tq
    row = pl.BlockSpec((1, nsub * tq, SB_W), lambda b, i: (b, i, 0))
    const = lambda a: pl.BlockSpec(a.shape, lambda b, i: (0, 0), pipeline_mode=pl.Buffered(1))
    ww, wb = _suffix_weights(tw), _suffix_weights(SB_KBLOCK)
    cmr = jnp.asarray(np.arange(tw)[None, :] - (np.arange(SB_HEADS * tq) % tq)[:, None], jnp.int32)
    if past_is_self:
        full = pl.BlockSpec((1, 1, P, SB_W), lambda b, i: (0, b, 0, 0), pipeline_mode=pl.Buffered(1))
        args = [sq, k_past, v_past, ww, wb, cmr]
        specs = [row, full, full, const(ww), const(wb), const(cmr)]
        scratch = []
    else:
        assert P % wp == 0
        win = pl.BlockSpec((1, 1, SB_W, wp), lambda b, i: (layer, b, 0, P // wp - 1))
        hbm = pl.BlockSpec(memory_space=pl.ANY)
        args = [sq, k_new, v_new, k_past, v_past, k_past, v_past, ww, wb, cmr]
        specs = [row, row, row, win, win, hbm, hbm, const(ww), const(wb), const(cmr)]
        scratch = [pltpu.VMEM((SB_W, SB_KBLOCK), F32), pltpu.VMEM((SB_W, SB_KBLOCK), F32),
                   pltpu.SemaphoreType.DMA((2,))]
    return pl.pallas_call(
        functools.partial(_sb_kernel, tq, nsub, tw, wp, P, layer, past_is_self),
        grid=(B, n),
        in_specs=specs,
        out_specs=row,
        out_shape=jax.ShapeDtypeStruct((B, T, SB_W), BF16),
        scratch_shapes=scratch,
        compiler_params=pltpu.CompilerParams(dimension_semantics=("arbitrary", "arbitrary"),
                                             vmem_limit_bytes=VMEM_LIMIT_BYTES),
        name="sb",
    )(*args)


FF_CHUNK = 1024


def _post_kernel(pre_ln, x_ref, go_ref, so_ref, co_ref, lng_ref, lnb_ref, wo_ref, g1_ref, b1_ref,
                 wu_ref, wd_ref, g2_ref, b2_ref, y_ref):
    x = x_ref[...]
    if pre_ln:
        x = _layer_norm(x, lng_ref[...], lnb_ref[...])
    mix = (jnp.dot(go_ref[...], wo_ref[0:GLA_W, :], preferred_element_type=F32)
           + jnp.dot(so_ref[...], wo_ref[GLA_W:GLA_W + SB_W, :], preferred_element_type=F32)
           + jnp.dot(co_ref[...], wo_ref[GLA_W + SB_W:MIX_W, :], preferred_element_type=F32))
    x1 = _layer_norm(DEEPNORM_ALPHA * x + mix, g1_ref[...], b1_ref[...])
    x1b = x1.astype(BF16)
    ff = None
    for c0 in range(0, D_FF, FF_CHUNK):
        h = jnp.dot(x1b, wu_ref[:, c0:c0 + FF_CHUNK], preferred_element_type=F32)
        h = jnp.square(jnp.maximum(h, 0.0)).astype(BF16)
        part = jnp.dot(h, wd_ref[c0:c0 + FF_CHUNK, :], preferred_element_type=F32)
        ff = part if ff is None else ff + part
    y_ref[...] = _layer_norm(DEEPNORM_ALPHA * x1 + ff, g2_ref[...], b2_ref[...])


def _post(x, go, so, co, pre_ln, ln_g, ln_b, wo, g1, b1, wu, wd, g2, b2):
    N, D = x.shape
    tm = min(512, N)
    row = lambda width: pl.BlockSpec((tm, width), lambda i: (i, 0))
    const = lambda a: pl.BlockSpec(a.shape, lambda i: (0, 0), pipeline_mode=pl.Buffered(1))
    return pl.pallas_call(
        functools.partial(_post_kernel, pre_ln),
        grid=(N // tm,),
        in_specs=[row(D), row(GLA_W), row(SB_W), row(CONV_CH), const(ln_g), const(ln_b), const(wo),
                  const(g1), const(b1), const(wu), const(wd), const(g2), const(b2)],
        out_specs=row(D),
        out_shape=jax.ShapeDtypeStruct((N, D), F32),
        compiler_params=pltpu.CompilerParams(dimension_semantics=("arbitrary",),
                                             vmem_limit_bytes=VMEM_LIMIT_BYTES),
        name="post",
    )(x, go, so, co, ln_g, ln_b, wo, g1, b1, wu, wd, g2, b2)


def _prep_layer(l, w_in, w_gate_up, b_gate, gla_norm_g, conv_w, w_out, ln1_g, ln1_b, w_up, w_down,
                ln2_g, ln2_b):
    wl = w_in[l]
    w = jnp.concatenate([wl[:, :_GLOW_SRC], wl[:, _GLOW_SRC + GLA_GATE_RANK:],
                         wl[:, _GLOW_SRC:_GLOW_SRC + GLA_GATE_RANK],
                         jnp.zeros((D_MODEL, LANES - GLA_GATE_RANK), F32)], axis=1).astype(BF16)
    wg = jnp.concatenate([w_gate_up[l], jnp.zeros((LANES - GLA_GATE_RANK, GLA_KW), F32)], axis=0).astype(BF16)
    r2 = lambda a: a.reshape(1, -1)
    return dict(w=w, wg=wg, bg=r2(b_gate[l]), gn=r2(gla_norm_g[l]), cw=conv_w[l],
                wo=w_out[l].astype(BF16), g1=r2(ln1_g[l]), b1=r2(ln1_b[l]),
                wu=w_up[l].astype(BF16), wd=w_down[l].astype(BF16), g2=r2(ln2_g[l]), b2=r2(ln2_b[l]))


def _trunk(x, c_k, c_v, s_gla, s_conv, ln_g, ln_b, layers):
    B, T, D = x.shape
    ks, vs, gs, cs = [], [], [], []
    for l, p in enumerate(layers):
        pre_ln = l == 0
        if c_k is None:
            s0 = jnp.zeros((B, GLA_HEADS, GLA_DK, GLA_DV), F32)
            cprev = jnp.zeros((B, CONV_W - 1, CONV_CH), F32)
        else:
            s0, cprev = s_gla[l], s_conv[l]
        gq, gk, gv, gr, lg, sq, skf, svf, skb, svb, co, cst = _in_proj(
            x, pre_ln, ln_g, ln_b, p["w"], p["wg"], p["bg"], p["cw"], cprev)
        go, s_fin = _gla(gq, gk, gv, lg, gr, p["gn"], s0)
        if c_k is None:
            so = _sb(sq, skb, svb, skb[None], svb[None], 0, True)
        else:
            key_minor = lambda c: jnp.transpose(c, (0, 1, 3, 4, 2)).reshape(DEPTH, B, SB_W, -1)
            so = _sb(sq, skb, svb, key_minor(c_k), key_minor(c_v), l, False)
        x = _post(x.reshape(B * T, D), go.reshape(B * T, GLA_W), so.reshape(B * T, SB_W),
                  co.reshape(B * T, CONV_CH), pre_ln, ln_g, ln_b, p["wo"], p["g1"], p["b1"],
                  p["wu"], p["wd"], p["g2"], p["b2"]).reshape(B, T, D)
        ks.append(skf)
        vs.append(svf)
        gs.append(s_fin)
        cs.append(cst)
    token_major = lambda a: jnp.transpose(jnp.stack(a).reshape(DEPTH, B, SB_HEADS, SB_DH, T), (0, 1, 4, 2, 3))
    return x, token_major(ks), token_major(vs), jnp.stack(gs), jnp.stack(cs)


def kernel(x_prompt, x_sample, cache_sb_k, cache_sb_v, state_gla, state_conv, ln_in_g, ln_in_b, w_in,
           w_gate_up, b_gate, gla_norm_g, conv_w, w_out, ln1_g, ln1_b, w_up, w_down, ln2_g, ln2_b):
    layers = [_prep_layer(l, w_in, w_gate_up, b_gate, gla_norm_g, conv_w, w_out, ln1_g, ln1_b, w_up,
                          w_down, ln2_g, ln2_b) for l in range(DEPTH)]
    ln_g, ln_b = ln_in_g.reshape(1, -1), ln_in_b.reshape(1, -1)
    y_p, k_p, v_p, gla_p, conv_p = _trunk(x_prompt, None, None, None, None, ln_g, ln_b, layers)
    y_s, k_s, v_s, gla_s, conv_s = _trunk(x_sample, cache_sb_k, cache_sb_v, state_gla, state_conv,
                                          ln_g, ln_b, layers)
    return (y_p, y_s, k_p, v_p, gla_p, conv_p, k_s, v_s, gla_s, conv_s)
```

```python
import functools

import numpy as np

import jax
import jax.numpy as jnp
from jax import lax
from jax.experimental import pallas as pl
from jax.experimental.pallas import tpu as pltpu

F32 = jnp.float32
BF16 = jnp.bfloat16

D_MODEL = 1024
DEPTH = 2
CHUNK = 64
GLA_HEADS = 4
GLA_DK = 64
GLA_DV = 128
GLA_GATE_RANK = 16
GLA_TAU = 16.0
GLA_W = GLA_HEADS * GLA_DV
GLA_KW = GLA_HEADS * GLA_DK
SB_HEADS = 4
SB_DH = 64
SB_W = SB_HEADS * SB_DH
CONV_CH = 256
CONV_W = 3
MIX_W = GLA_W + SB_W + CONV_CH
D_FF = 4 * D_MODEL
LN_EPS = 1e-5
DEEPNORM_ALPHA = (2 * DEPTH) ** 0.25

LANES = 128
SUBLANES = 8
SB_KBLOCK = 128
SB_DEAD_LOG = -104.0
VMEM_LIMIT_BYTES = 56 * 1024 * 1024

_C_GQ = 0
_C_GK = _C_GQ + GLA_KW
_C_GV = _C_GK + GLA_KW
_C_GR = _C_GV + GLA_W
_C_SQ = _C_GR + GLA_W
_C_SK = _C_SQ + SB_W
_C_SV = _C_SK + SB_W
_C_CB = _C_SV + SB_W
_C_CC = _C_CB + CONV_CH
_C_CH = _C_CC + CONV_CH
_C_GLOW = _C_CH + CONV_CH
N_IN_PAD = _C_GLOW + LANES
_GLOW_SRC = 2 * GLA_KW + 2 * GLA_W


def _layer_norm(x, g, b):
    mu = jnp.mean(x, axis=-1, keepdims=True)
    xc = x - mu
    var = jnp.mean(xc * xc, axis=-1, keepdims=True)
    return xc * lax.rsqrt(var + LN_EPS) * g + b


def _softplus_neg_abs(z):
    return jnp.log1p(jnp.exp(-jnp.abs(z)))


def _split_bf16(x, n):
    parts = []
    r = x
    for _ in range(n - 1):
        p = r.astype(BF16)
        parts.append(p)
        r = r - p.astype(F32)
    parts.append(r.astype(BF16))
    return parts


def _in_proj_kernel(pre_ln, tm, x_ref, lng_ref, lnb_ref, w_ref, wg_ref, bg_ref, cw_ref, cprev_ref,
                    gq_ref, gk_ref, gv_ref, gr_ref, lg_ref, sq_ref, skf_ref, svf_ref, skb_ref, svb_ref,
                    co_ref, cst_ref, uext_ref):
    i = pl.program_id(1)

    @pl.when(i == 0)
    def _():
        uext_ref[SUBLANES - 2:SUBLANES, :] = cprev_ref[0]

    x = x_ref[0]
    if pre_ln:
        x = _layer_norm(x, lng_ref[...], lnb_ref[...])
    xb = x.astype(BF16)

    def proj(a, b):
        return jnp.dot(xb, w_ref[:, a:b], preferred_element_type=F32)

    glow = proj(_C_GLOW, N_IN_PAD)
    u = jnp.dot(glow.astype(BF16), wg_ref[...], preferred_element_type=F32) + bg_ref[...]
    lg_ref[0] = (jnp.minimum(u, 0.0) - _softplus_neg_abs(u)) * (1.0 / GLA_TAU)

    cb = proj(_C_CB, _C_CC)
    uc = proj(_C_CC, _C_CH) * proj(_C_CH, _C_GLOW)
    uext_ref[SUBLANES:SUBLANES + tm, :] = uc
    um1 = uext_ref[SUBLANES - 1:SUBLANES - 1 + tm, :]
    um2 = uext_ref[SUBLANES - 2:SUBLANES - 2 + tm, :]
    cw = cw_ref[...]
    cy = (cw[2:3, :] * uc + cw[0:1, :] * um2) + cw[1:2, :] * um1
    co_ref[0] = (cb * cy).astype(BF16)
    tail = uc[tm - 2:tm, :]
    uext_ref[SUBLANES - 2:SUBLANES, :] = tail

    gq_ref[0] = proj(_C_GQ, _C_GK) * (GLA_DK ** -0.5)
    gk_ref[0] = proj(_C_GK, _C_GV)
    gv_ref[0] = proj(_C_GV, _C_GR).astype(BF16)
    gr_ref[0] = proj(_C_GR, _C_SQ)
    sq_ref[0] = (proj(_C_SQ, _C_SK) * (SB_DH ** -0.5)).astype(BF16)
    sk = proj(_C_SK, _C_SV)
    skf_ref[0] = sk.T
    skb_ref[0] = sk.astype(BF16)
    sv = proj(_C_SV, _C_CB)
    svf_ref[0] = sv.T
    svb_ref[0] = sv.astype(BF16)

    @pl.when(i == pl.num_programs(1) - 1)
    def _():
        cst_ref[0] = tail


def _in_proj(x, pre_ln, ln_g, ln_b, w, wg, bg, cw, cprev):
    B, T, D = x.shape
    tm = min(512, T)
    n = T // tm
    row = lambda width: pl.BlockSpec((1, tm, width), lambda b, i: (b, i, 0))
    const2 = lambda a: pl.BlockSpec(a.shape, lambda b, i: (0, 0), pipeline_mode=pl.Buffered(1))
    out_shapes = [
        jax.ShapeDtypeStruct((B, T, GLA_KW), F32),
        jax.ShapeDtypeStruct((B, T, GLA_KW), F32),
        jax.ShapeDtypeStruct((B, T, GLA_W), BF16),
        jax.ShapeDtypeStruct((B, T, GLA_W), F32),
        jax.ShapeDtypeStruct((B, T, GLA_KW), F32),
        jax.ShapeDtypeStruct((B, T, SB_W), BF16),
        jax.ShapeDtypeStruct((B, SB_W, T), F32),
        jax.ShapeDtypeStruct((B, SB_W, T), F32),
        jax.ShapeDtypeStruct((B, T, SB_W), BF16),
        jax.ShapeDtypeStruct((B, T, SB_W), BF16),
        jax.ShapeDtypeStruct((B, T, CONV_CH), BF16),
        jax.ShapeDtypeStruct((B, CONV_W - 1, CONV_CH), F32),
    ]
    out_specs = [row(s.shape[-1]) for s in out_shapes[:-1]]
    out_specs[6] = out_specs[7] = pl.BlockSpec((1, SB_W, tm), lambda b, i: (b, 0, i))
    out_specs.append(pl.BlockSpec((1, CONV_W - 1, CONV_CH), lambda b, i: (b, 0, 0)))
    return pl.pallas_call(
        functools.partial(_in_proj_kernel, pre_ln, tm),
        grid=(B, n),
        in_specs=[row(D), const2(ln_g), const2(ln_b), const2(w), const2(wg), const2(bg), const2(cw),
                  pl.BlockSpec((1, CONV_W - 1, CONV_CH), lambda b, i: (b, 0, 0))],
        out_specs=out_specs,
        out_shape=out_shapes,
        scratch_shapes=[pltpu.VMEM((SUBLANES + tm, CONV_CH), F32)],
        compiler_params=pltpu.CompilerParams(dimension_semantics=("arbitrary", "arbitrary"),
                                             vmem_limit_bytes=VMEM_LIMIT_BYTES),
        name="in_proj",
    )(x, ln_g, ln_b, w, wg, bg, cw, cprev)


GLA_LEVELS = (32, 16, 8, 4, 2, 1)
ATT_W = GLA_HEADS * CHUNK
B_PAD = SUBLANES
(_K_NEG_UP, _K_NEG_LO, _K_PAIR) = (0, len(GLA_LEVELS), 2 * len(GLA_LEVELS))
_K_EYE = 3 * len(GLA_LEVELS)
_K_POS4 = _K_EYE + 1
_K_ODD = _K_POS4 + 4
_K_COUNT = _K_ODD + 1


def _gla_constants():
    assert ATT_W == GLA_KW
    t = np.arange(CHUNK)[:, None]
    s = (np.arange(ATT_W) % CHUNK)[None, :]
    ninf = np.float32(-np.inf)
    out = np.zeros((_K_COUNT, CHUNK, ATT_W), np.float32)
    for li, h in enumerate(GLA_LEVELS):
        upper = np.broadcast_to((t & h) != 0, (CHUNK, ATT_W))
        out[_K_NEG_UP + li] = np.where(upper, 0.0, ninf)
        out[_K_NEG_LO + li] = np.where(upper, ninf, 0.0)
        out[_K_PAIR + li] = (((t ^ s) // (2 * h)) == 0) & upper & ((s & h) == 0)
    out[_K_EYE] = t == s
    for p in range(4):
        out[_K_POS4 + p] = np.broadcast_to((t & 3) == p, (CHUNK, ATT_W))
    out[_K_ODD] = np.broadcast_to((t & 1) == 1, (CHUNK, ATT_W))
    tril = (np.arange(CHUNK)[None, :] <= np.arange(CHUNK)[:, None]).astype(np.float32)
    hk = (np.arange(ATT_W)[:, None] // CHUNK == np.arange(GLA_KW)[None, :] // GLA_DK).astype(np.float32)
    hv = (np.arange(ATT_W)[:, None] // CHUNK == np.arange(GLA_W)[None, :] // GLA_DV).astype(np.float32)
    return (jnp.asarray(out), jnp.asarray(tril, BF16), jnp.asarray(hk, BF16), jnp.asarray(hv, BF16))


def _gla_kernel(nc, q_ref, k_ref, v_ref, lg_ref, gr_ref, gn_ref, s0_ref, cst_ref, tril_ref, hk_ref,
                hv_ref, go_ref, sfin_ref, st_ref, b_ref):
    i = pl.program_id(1)
    C = CHUNK
    T = nc * C
    nt = (((1,), (1,)), ((), ()))
    tn = (((0,), (0,)), ((), ()))
    bnn = (((2,), (1,)), ((0,), (0,)))
    bnt = (((2,), (2,)), ((0,), (0,)))

    @pl.when(i == 0)
    def _():
        for h in range(GLA_HEADS):
            st_ref[h] = s0_ref[0, h].T
        b_ref[...] = jnp.zeros(b_ref.shape, F32)

    def chunks(a):
        return a.reshape(nc, C, a.shape[-1])

    q2, k2 = q_ref[0], k_ref[0]
    q, k, v = chunks(q2), chunks(k2), chunks(v_ref[0])
    tril = jnp.broadcast_to(tril_ref[...], (nc, C, C))
    b = sum(lax.dot_general(tril, chunks(p), bnn, preferred_element_type=F32)
            for p in _split_bf16(lg_ref[0], 3))
    b_ref[pl.ds(B_PAD, T), :] = b.reshape(T, GLA_KW)
    b_last = b[:, C - 1:C, :]
    qe = (q * jnp.exp(b)).astype(BF16)
    kd = (k * jnp.exp(b_last - b)).astype(BF16)
    e_last = jnp.exp(b_last)

    def reference_rows(h):
        if h >= 4:
            pieces = [jnp.broadcast_to(b[:, m * 2 * h + h - 1:m * 2 * h + h, :], (nc, 2 * h, GLA_KW))
                      for m in range(C // (2 * h))]
            return pieces[0] if len(pieces) == 1 else jnp.concatenate(pieces, axis=1)
        bm1 = chunks(b_ref[pl.ds(B_PAD - 1, T), :])
        if h == 1:
            return b + cst_ref[_K_ODD] * (bm1 - b)
        bm2 = chunks(b_ref[pl.ds(B_PAD - 2, T), :])
        bp1 = chunks(b_ref[pl.ds(B_PAD + 1, T), :])
        return ((cst_ref[_K_POS4 + 0] * bp1 + cst_ref[_K_POS4 + 1] * b)
                + (cst_ref[_K_POS4 + 2] * bm1 + cst_ref[_K_POS4 + 3] * bm2))

    hk = hk_ref[...]
    att = chunks(jnp.dot((q2 * k2).astype(BF16), hk, preferred_element_type=F32)) * cst_ref[_K_EYE]
    for li, h in enumerate(GLA_LEVELS):
        r = reference_rows(h)
        qt = (q * jnp.exp((b - r) + cst_ref[_K_NEG_UP + li])).astype(BF16)
        kt = (k * jnp.exp((r - b) + cst_ref[_K_NEG_LO + li])).astype(BF16)
        kbd = jnp.concatenate([kt] * GLA_HEADS, axis=1) * hk
        att = att + lax.dot_general(qt, kbd, bnt, preferred_element_type=F32) * cst_ref[_K_PAIR + li]
    vbd = jnp.concatenate([v] * GLA_HEADS, axis=1) * hv_ref[...]
    o_intra = lax.dot_general(att.astype(BF16), vbd, bnn, preferred_element_type=F32)

    st = [st_ref[h] for h in range(GLA_HEADS)]
    o_inter = []
    for c in range(nc):
        per_head = []
        for h in range(GLA_HEADS):
            ks = slice(h * GLA_DK, (h + 1) * GLA_DK)
            vs = slice(h * GLA_DV, (h + 1) * GLA_DV)
            per_head.append(lax.dot_general(qe[c][:, ks], st[h].astype(BF16), nt,
                                            preferred_element_type=F32))
            st[h] = st[h] * e_last[c][:, ks] + lax.dot_general(v[c][:, vs], kd[c][:, ks], tn,
                                                               preferred_element_type=F32)
        o_inter.append(jnp.concatenate(per_head, axis=1))
    for h in range(GLA_HEADS):
        st_ref[h] = st[h]
    o = o_intra.reshape(T, GLA_W) + jnp.concatenate(o_inter, axis=0)

    normed = []
    for h in range(GLA_HEADS):
        oh = o[:, h * GLA_DV:(h + 1) * GLA_DV]
        normed.append(oh * lax.rsqrt(jnp.mean(oh * oh, axis=-1, keepdims=True) + LN_EPS))
    gr = gr_ref[0]
    out = jnp.concatenate(normed, axis=1) * gn_ref[...] * (gr * jax.nn.sigmoid(gr))
    go_ref[0] = out.astype(BF16)

    @pl.when(i == pl.num_programs(1) - 1)
    def _():
        for h in range(GLA_HEADS):
            sfin_ref[0, h] = st_ref[h].T


def _gla(gq, gk, gv, lg, gr, gn, s0):
    B, T, _ = gq.shape
    tg = min(1024, T)
    n = T // tg
    row = lambda width: pl.BlockSpec((1, tg, width), lambda b, i: (b, i, 0))
    state = pl.BlockSpec((1, GLA_HEADS, GLA_DK, GLA_DV), lambda b, i: (b, 0, 0, 0))
    const = lambda a: pl.BlockSpec(a.shape, lambda b, i: (0,) * a.ndim, pipeline_mode=pl.Buffered(1))
    consts = _gla_constants()
    return pl.pallas_call(
        functools.partial(_gla_kernel, tg // CHUNK),
        grid=(B, n),
        in_specs=[row(GLA_KW), row(GLA_KW), row(GLA_W), row(GLA_KW), row(GLA_W), const(gn), state]
        + [const(c) for c in consts],
        out_specs=[row(GLA_W), state],
        out_shape=[jax.ShapeDtypeStruct((B, T, GLA_W), BF16),
                   jax.ShapeDtypeStruct((B, GLA_HEADS, GLA_DK, GLA_DV), F32)],
        scratch_shapes=[pltpu.VMEM((GLA_HEADS, GLA_DV, GLA_DK), F32),
                        pltpu.VMEM((tg + 2 * B_PAD, GLA_KW), F32)],
        compiler_params=pltpu.CompilerParams(dimension_semantics=("arbitrary", "arbitrary"),
                                             vmem_limit_bytes=VMEM_LIMIT_BYTES),
        name="gla",
    )(gq, gk, gv, lg, gr, gn, s0, *consts)


SB_WINDOW_PAST = 256


def _suffix_weights():
    j = (np.arange(2 * LANES) % LANES)[:, None]
    s = np.arange(2 * LANES)[None, :]
    return jnp.asarray((j > s) | (s >= LANES), BF16)


SB_MASKED_LOGIT = -1e30


def _sb_step(q_tiles, kws, vws, w, prefix, lims, col_minus_row):
    tq, tw = q_tiles[0].shape[0], kws[0].shape[0]
    n_tiles = len(q_tiles)
    nt = (((1,), (1,)), ((), ()))
    heads = [slice(h * SB_DH, (h + 1) * SB_DH) for h in range(SB_HEADS)]
    rows = SB_HEADS * tq
    st = [dict() for _ in range(n_tiles)]

    def logits(u):
        z = jnp.concatenate([lax.dot_general(q_tiles[u][:, hs], kws[u][:, hs], nt,
                                             preferred_element_type=F32) for hs in heads], axis=0)
        if lims is not None:
            z = jnp.where(col_minus_row < lims[u], z, SB_MASKED_LOGIT)
        st[u]["z"] = z

    def logs(u):
        z = st[u].pop("z")
        log_beta = jnp.minimum(z, 0.0) - jnp.log(1.0 + jnp.exp(-jnp.abs(z)))
        st[u]["log_beta"] = log_beta
        st[u]["keep"] = _split_bf16(log_beta - z, 2)

    def suffix(u):
        hi, lo = st[u].pop("keep")
        before = None if prefix is None else prefix[u * rows:(u + 1) * rows]
        after = [None] * (tw // LANES)
        total = None
        for blk in reversed(range(tw // LANES)):
            cs = slice(blk * LANES, (blk + 1) * LANES)
            res = jnp.dot(jnp.concatenate([hi[:, cs], lo[:, cs]], axis=1), w, preferred_element_type=F32)
            later = total if before is None else (before if total is None else before + total)
            after[blk] = res[:, :LANES] if later is None else res[:, :LANES] + later
            total = res[:, LANES:] if total is None else total + res[:, LANES:]
        st[u]["after"] = after
        st[u]["total"] = total

    def weights(u):
        after = st[u].pop("after")
        after = after[0] if len(after) == 1 else jnp.concatenate(after, axis=1)
        st[u]["a"] = jnp.exp(st[u].pop("log_beta") + after).astype(BF16)

    def values(u):
        a = st[u].pop("a")
        st[u]["out"] = [jnp.dot(a[h * tq:(h + 1) * tq], vws[u][:, hs], preferred_element_type=F32)
                        for h, hs in enumerate(heads)]

    stages = (logits, logs, suffix, weights, values)
    for step in range(n_tiles + len(stages) - 1):
        for depth in reversed(range(len(stages))):
            if 0 <= step - depth < n_tiles:
                stages[depth](step - depth)
    totals = [s["total"] for s in st]
    return [s["out"] for s in st], totals[0] if n_tiles == 1 else jnp.concatenate(totals, axis=0)


def _sb_kernel(tq, nsub, tw, wp, n_past, layer, past_is_self, *refs):
    if past_is_self:
        q_ref, kp_ref, vp_ref, w_ref, cmr_ref, o_ref = refs
        kp_ref = kp_ref.at[0, 0]
        vp_ref = vp_ref.at[0, 0]
    else:
        (q_ref, kd_ref, vd_ref, kwin_ref, vwin_ref, kc_hbm, vc_hbm, w_ref, cmr_ref, o_ref,
         kbuf_ref, vbuf_ref, sem) = refs
    i = pl.program_id(1)
    q_tiles = [q_ref[0, u * tq:(u + 1) * tq, :] for u in range(nsub)]

    if past_is_self:
        q0s = [(i * nsub + u) * tq for u in range(nsub)]
        rests = [pl.multiple_of(jnp.maximum(q0 - wp, 0), SB_KBLOCK) for q0 in q0s]
        kws = [kp_ref[pl.ds(r, tw), :] for r in rests]
        vws = [vp_ref[pl.ds(r, tw), :] for r in rests]
        lims = [q0 - r for q0, r in zip(q0s, rests)]
    else:
        rests = [n_past - wp]
        pad = jnp.zeros((tw - wp - tq, SB_W), BF16)
        kws = [jnp.concatenate([kwin_ref[0, 0].T.astype(BF16), kd_ref[0], pad], axis=0)]
        vws = [jnp.concatenate([vwin_ref[0, 0].T.astype(BF16), vd_ref[0], pad], axis=0)]
        lims = [wp]

    def past_block(start):
        if past_is_self:
            return kp_ref[pl.ds(start, SB_KBLOCK), :], vp_ref[pl.ds(start, SB_KBLOCK), :]
        b = pl.program_id(0)
        copies = [pltpu.make_async_copy(src.at[layer, b, :, pl.ds(start, SB_KBLOCK)], dst, sem.at[n])
                  for n, (src, dst) in enumerate(((kc_hbm, kbuf_ref), (vc_hbm, vbuf_ref)))]
        for cp in copies:
            cp.start()
        for cp in copies:
            cp.wait()
        return kbuf_ref[...].T.astype(BF16), vbuf_ref[...].T.astype(BF16)
    outs, prefs = _sb_step(q_tiles, kws, vws, w_ref[...], None, lims, cmr_ref[...])

    def store(u, heads_out):
        o_ref[0, u * tq:(u + 1) * tq, :] = jnp.concatenate(heads_out, axis=1).astype(BF16)

    rows = SB_HEADS * tq
    nblks = [r // SB_KBLOCK if isinstance(r, int) else lax.shift_right_logical(r, 7) for r in rests]
    alive = []
    for u in range(nsub):
        store(u, outs[u])
        alive.append(jnp.logical_and(nblks[u] > 0, jnp.max(prefs[u * rows:(u + 1) * rows]) > SB_DEAD_LOG))

    @pl.when(functools.reduce(jnp.logical_or, alive))
    def _():
        for u in range(nsub):
            def cond(carry, nblk=nblks[u]):
                return jnp.logical_and(carry[0] < nblk, carry[1])

            def body(carry, rest=rests[u], u=u):
                j, pref = carry[0], carry[2]
                start = pl.multiple_of(rest - (j + 1) * SB_KBLOCK, SB_KBLOCK)
                kb, vb = past_block(start)
                o_new, p_new = _sb_step([q_tiles[u]], [kb], [vb], w_ref[...], pref, None, None)
                pref = pref + p_new
                return (j + 1, jnp.max(pref) > SB_DEAD_LOG, pref,
                        *[o + n for o, n in zip(carry[3:], o_new[0])])

            final = lax.while_loop(cond, body, (jnp.int32(0), alive[u], prefs[u * rows:(u + 1) * rows],
                                                *outs[u]))
            store(u, final[3:])


def _sb(sq, k_new, v_new, k_past, v_past, layer, past_is_self):
    B, T, _ = sq.shape
    P = k_past.shape[2] if past_is_self else k_past.shape[3]
    tq = min(SB_KBLOCK, T)
    nsub = 4 if past_is_self and T % (4 * tq) == 0 else 1
    n = T // (nsub * tq)
    wp = min(SB_WINDOW_PAST, P - tq if past_is_self else P)
    tw = -(-(wp + tq) // LANES) * LANES
    assert T % tq == 0 and tq & (tq - 1) == 0 and P % SB_KBLOCK == 0 and wp % SB_KBLOCK == 0
    assert T == tq or tq == SB_KBLOCK
    assert not past_is_self or tw == wp + tq
    row = pl.BlockSpec((1, nsub * tq, SB_W), lambda b, i: (b, i, 0))
    const = lambda a: pl.BlockSpec(a.shape, lambda b, i: (0, 0), pipeline_mode=pl.Buffered(1))
    w = _suffix_weights()
    cmr = jnp.asarray(np.arange(tw)[None, :] - (np.arange(SB_HEADS * tq) % tq)[:, None], jnp.int32)
    if past_is_self:
        full = pl.BlockSpec((1, 1, P, SB_W), lambda b, i: (0, b, 0, 0), pipeline_mode=pl.Buffered(1))
        args = [sq, k_past, v_past, w, cmr]
        specs = [row, full, full, const(w), const(cmr)]
        scratch = []
    else:
        assert P % wp == 0
        win = pl.BlockSpec((1, 1, SB_W, wp), lambda b, i: (layer, b, 0, P // wp - 1))
        hbm = pl.BlockSpec(memory_space=pl.ANY)
        args = [sq, k_new, v_new, k_past, v_past, k_past, v_past, w, cmr]
        specs = [row, row, row, win, win, hbm, hbm, const(w), const(cmr)]
        scratch = [pltpu.VMEM((SB_W, SB_KBLOCK), F32), pltpu.VMEM((SB_W, SB_KBLOCK), F32),
                   pltpu.SemaphoreType.DMA((2,))]
    return pl.pallas_call(
        functools.partial(_sb_kernel, tq, nsub, tw, wp, P, layer, past_is_self),
        grid=(B, n),
        in_specs=specs,
        out_specs=row,
        out_shape=jax.ShapeDtypeStruct((B, T, SB_W), BF16),
        scratch_shapes=scratch,
        compiler_params=pltpu.CompilerParams(dimension_semantics=("arbitrary", "arbitrary"),
                                             vmem_limit_bytes=VMEM_LIMIT_BYTES),
        name="sb",
    )(*args)


FF_CHUNK = 1024


def _post_kernel(pre_ln, x_ref, go_ref, so_ref, co_ref, lng_ref, lnb_ref, wo_ref, g1_ref, b1_ref,
                 wu_ref, wd_ref, g2_ref, b2_ref, y_ref):
    tm = x_ref.shape[0]
    n_ff = D_FF // FF_CHUNK
    if tm % (2 * n_ff * SUBLANES) == 0:
        halves = [(0, tm // 2), (tm // 2, tm // 2)]
    else:
        halves = [(0, tm)]

    def mix_rows(r0, n):
        rs = slice(r0, r0 + n)
        return (jnp.dot(go_ref[rs, :], wo_ref[0:GLA_W, :], preferred_element_type=F32)
                + jnp.dot(so_ref[rs, :], wo_ref[GLA_W:GLA_W + SB_W, :], preferred_element_type=F32)
                + jnp.dot(co_ref[rs, :], wo_ref[GLA_W + SB_W:MIX_W, :], preferred_element_type=F32))

    def ln1_rows(mix, r0, n, off):
        x = x_ref[r0 + off:r0 + off + n, :]
        if pre_ln:
            x = _layer_norm(x, lng_ref[...], lnb_ref[...])
        return _layer_norm(DEEPNORM_ALPHA * x + mix[off:off + n], g1_ref[...], b1_ref[...])

    def ffn_chunk(x1b, c):
        c0 = c * FF_CHUNK
        h = jnp.dot(x1b, wu_ref[:, c0:c0 + FF_CHUNK], preferred_element_type=F32)
        h = jnp.square(jnp.maximum(h, 0.0)).astype(BF16)
        return jnp.dot(h, wd_ref[c0:c0 + FF_CHUNK, :], preferred_element_type=F32)

    def ln2_rows(x1, ff, r0, n, off):
        y_ref[r0 + off:r0 + off + n, :] = _layer_norm(DEEPNORM_ALPHA * x1[off:off + n] + ff[off:off + n],
                                                     g2_ref[...], b2_ref[...])

    r0, n = halves[0]
    x1 = ln1_rows(mix_rows(r0, n), r0, n, 0)
    prev = None
    for hi in range(len(halves)):
        r0, n = halves[hi]
        nxt = halves[hi + 1] if hi + 1 < len(halves) else None
        nxt_mix = mix_rows(*nxt) if nxt else None
        x1b = x1.astype(BF16)
        g = n // n_ff
        ff, nxt_x1 = None, []
        for c in range(n_ff):
            part = ffn_chunk(x1b, c)
            ff = part if ff is None else ff + part
            if prev is not None:
                ln2_rows(prev[0], prev[1], prev[2], g, c * g)
            if nxt:
                nxt_x1.append(ln1_rows(nxt_mix, nxt[0], g, c * g))
        prev = (x1, ff, r0, n)
        if nxt:
            x1 = jnp.concatenate(nxt_x1, axis=0)
    ln2_rows(prev[0], prev[1], prev[2], prev[3], 0)


def _post(x, go, so, co, pre_ln, ln_g, ln_b, wo, g1, b1, wu, wd, g2, b2):
    N, D = x.shape
    tm = min(512, N)
    row = lambda width: pl.BlockSpec((tm, width), lambda i: (i, 0))
    const = lambda a: pl.BlockSpec(a.shape, lambda i: (0, 0), pipeline_mode=pl.Buffered(1))
    return pl.pallas_call(
        functools.partial(_post_kernel, pre_ln),
        grid=(N // tm,),
        in_specs=[row(D), row(GLA_W), row(SB_W), row(CONV_CH), const(ln_g), const(ln_b), const(wo),
                  const(g1), const(b1), const(wu), const(wd), const(g2), const(b2)],
        out_specs=row(D),
        out_shape=jax.ShapeDtypeStruct((N, D), F32),
        compiler_params=pltpu.CompilerParams(dimension_semantics=("arbitrary",),
                                             vmem_limit_bytes=VMEM_LIMIT_BYTES),
        name="post",
    )(x, go, so, co, ln_g, ln_b, wo, g1, b1, wu, wd, g2, b2)


def _prep_layer(l, w_in, w_gate_up, b_gate, gla_norm_g, conv_w, w_out, ln1_g, ln1_b, w_up, w_down,
                ln2_g, ln2_b):
    wl = w_in[l]
    w = jnp.concatenate([wl[:, :_GLOW_SRC], wl[:, _GLOW_SRC + GLA_GATE_RANK:],
                         wl[:, _GLOW_SRC:_GLOW_SRC + GLA_GATE_RANK],
                         jnp.zeros((D_MODEL, LANES - GLA_GATE_RANK), F32)], axis=1).astype(BF16)
    wg = jnp.concatenate([w_gate_up[l], jnp.zeros((LANES - GLA_GATE_RANK, GLA_KW), F32)], axis=0).astype(BF16)
    r2 = lambda a: a.reshape(1, -1)
    return dict(w=w, wg=wg, bg=r2(b_gate[l]), gn=r2(gla_norm_g[l]), cw=conv_w[l],
                wo=w_out[l].astype(BF16), g1=r2(ln1_g[l]), b1=r2(ln1_b[l]),
                wu=w_up[l].astype(BF16), wd=w_down[l].astype(BF16), g2=r2(ln2_g[l]), b2=r2(ln2_b[l]))


def _trunk(x, c_k, c_v, s_gla, s_conv, ln_g, ln_b, layers):
    B, T, D = x.shape
    ks, vs, gs, cs = [], [], [], []
    for l, p in enumerate(layers):
        pre_ln = l == 0
        if c_k is None:
            s0 = jnp.zeros((B, GLA_HEADS, GLA_DK, GLA_DV), F32)
            cprev = jnp.zeros((B, CONV_W - 1, CONV_CH), F32)
        else:
            s0, cprev = s_gla[l], s_conv[l]
        gq, gk, gv, gr, lg, sq, skf, svf, skb, svb, co, cst = _in_proj(
            x, pre_ln, ln_g, ln_b, p["w"], p["wg"], p["bg"], p["cw"], cprev)
        go, s_fin = _gla(gq, gk, gv, lg, gr, p["gn"], s0)
        if c_k is None:
            so = _sb(sq, skb, svb, skb[None], svb[None], 0, True)
        else:
            key_minor = lambda c: jnp.transpose(c, (0, 1, 3, 4, 2)).reshape(DEPTH, B, SB_W, -1)
            so = _sb(sq, skb, svb, key_minor(c_k), key_minor(c_v), l, False)
        x = _post(x.reshape(B * T, D), go.reshape(B * T, GLA_W), so.reshape(B * T, SB_W),
                  co.reshape(B * T, CONV_CH), pre_ln, ln_g, ln_b, p["wo"], p["g1"], p["b1"],
                  p["wu"], p["wd"], p["g2"], p["b2"]).reshape(B, T, D)
        ks.append(skf)
        vs.append(svf)
        gs.append(s_fin)
        cs.append(cst)
    token_major = lambda a: jnp.transpose(jnp.stack(a).reshape(DEPTH, B, SB_HEADS, SB_DH, T), (0, 1, 4, 2, 3))
    return x, token_major(ks), token_major(vs), jnp.stack(gs), jnp.stack(cs)


def kernel(x_prompt, x_sample, cache_sb_k, cache_sb_v, state_gla, state_conv, ln_in_g, ln_in_b, w_in,
           w_gate_up, b_gate, gla_norm_g, conv_w, w_out, ln1_g, ln1_b, w_up, w_down, ln2_g, ln2_b):
    layers = [_prep_layer(l, w_in, w_gate_up, b_gate, gla_norm_g, conv_w, w_out, ln1_g, ln1_b, w_up,
                          w_down, ln2_g, ln2_b) for l in range(DEPTH)]
    ln_g, ln_b = ln_in_g.reshape(1, -1), ln_in_b.reshape(1, -1)
    y_p, k_p, v_p, gla_p, conv_p = _trunk(x_prompt, None, None, None, None, ln_g, ln_b, layers)
    y_s, k_s, v_s, gla_s, conv_s = _trunk(x_sample, cache_sb_k, cache_sb_v, state_gla, state_conv,
                                          ln_g, ln_b, layers)
    return (y_p, y_s, k_p, v_p, gla_p, conv_p, k_s, v_s, gla_s, conv_s)
```

```python
import functools

import numpy as np

import jax
import jax.numpy as jnp
from jax import lax
from jax.experimental import pallas as pl
from jax.experimental.pallas import tpu as pltpu

F32 = jnp.float32
BF16 = jnp.bfloat16

D_MODEL = 1024
DEPTH = 2
CHUNK = 64
GLA_HEADS = 4
GLA_DK = 64
GLA_DV = 128
GLA_GATE_RANK = 16
GLA_TAU = 16.0
GLA_W = GLA_HEADS * GLA_DV
GLA_KW = GLA_HEADS * GLA_DK
SB_HEADS = 4
SB_DH = 64
SB_W = SB_HEADS * SB_DH
CONV_CH = 256
CONV_W = 3
MIX_W = GLA_W + SB_W + CONV_CH
D_FF = 4 * D_MODEL
LN_EPS = 1e-5
DEEPNORM_ALPHA = (2 * DEPTH) ** 0.25
LOG2_E = 1.4426950408889634

LANES = 128
SUBLANES = 8
SB_KBLOCK = 128
SB_DEAD_LOG = -104.0
VMEM_LIMIT_BYTES = 56 * 1024 * 1024

_C_GQ = 0
_C_GK = _C_GQ + GLA_KW
_C_GV = _C_GK + GLA_KW
_C_GR = _C_GV + GLA_W
_C_SQ = _C_GR + GLA_W
_C_SK = _C_SQ + SB_W
_C_SV = _C_SK + SB_W
_C_CB = _C_SV + SB_W
_C_CC = _C_CB + CONV_CH
_C_CH = _C_CC + CONV_CH
_C_GLOW = _C_CH + CONV_CH
N_IN_PAD = _C_GLOW + LANES
_GLOW_SRC = 2 * GLA_KW + 2 * GLA_W


def _layer_norm(x, g, b):
    mu = jnp.mean(x, axis=-1, keepdims=True)
    xc = x - mu
    var = jnp.mean(xc * xc, axis=-1, keepdims=True)
    return xc * lax.rsqrt(var + LN_EPS) * g + b


def _softplus_neg_abs(z):
    return jnp.log1p(jnp.exp(-jnp.abs(z)))


def _split_bf16(x, n):
    parts = []
    r = x
    for _ in range(n - 1):
        p = r.astype(BF16)
        parts.append(p)
        r = r - p.astype(F32)
    parts.append(r.astype(BF16))
    return parts


IN_PROJ_PART_ROWS = 256


def _in_proj_kernel(pre_ln, tm, x_ref, lng_ref, lnb_ref, w_ref, wg_ref, bg_ref, cw_ref, cprev_ref,
                    gq_ref, gk_ref, gv_ref, gr_ref, lg_ref, sq_ref, skf_ref, svf_ref, skb_ref, svb_ref,
                    co_ref, cst_ref, uext_ref):
    i = pl.program_id(1)

    @pl.when(i == 0)
    def _():
        uext_ref[SUBLANES - 2:SUBLANES, :] = cprev_ref[0]

    part = IN_PROJ_PART_ROWS if tm % IN_PROJ_PART_ROWS == 0 else tm

    def normed_rows(r0):
        x = x_ref[0, r0:r0 + part, :]
        if pre_ln:
            x = _layer_norm(x, lng_ref[...], lnb_ref[...])
        return x.astype(BF16)

    cw = cw_ref[...]
    xb = normed_rows(0)
    for r0 in range(0, tm, part):
        rs = slice(r0, r0 + part)

        def proj(a, b, xb=xb):
            return jnp.dot(xb, w_ref[:, a:b], preferred_element_type=F32)

        glow = proj(_C_GLOW, N_IN_PAD)
        u = jnp.dot(glow.astype(BF16), wg_ref[...], preferred_element_type=F32) + bg_ref[...]
        lg_ref[0, rs, :] = (jnp.minimum(u, 0.0) - _softplus_neg_abs(u)) * (LOG2_E / GLA_TAU)

        cb = proj(_C_CB, _C_CC)
        uc = proj(_C_CC, _C_CH) * proj(_C_CH, _C_GLOW)
        uext_ref[SUBLANES + r0:SUBLANES + r0 + part, :] = uc
        um1 = uext_ref[SUBLANES - 1 + r0:SUBLANES - 1 + r0 + part, :]
        um2 = uext_ref[SUBLANES - 2 + r0:SUBLANES - 2 + r0 + part, :]
        cy = (cw[2:3, :] * uc + cw[0:1, :] * um2) + cw[1:2, :] * um1
        co_ref[0, rs, :] = (cb * cy).astype(BF16)

        gq_ref[0, rs, :] = proj(_C_GQ, _C_GK) * (GLA_DK ** -0.5)
        gk_ref[0, rs, :] = proj(_C_GK, _C_GV)
        gv_ref[0, rs, :] = proj(_C_GV, _C_GR).astype(BF16)
        xb_next = normed_rows(r0 + part) if r0 + part < tm else None
        gr_ref[0, rs, :] = proj(_C_GR, _C_SQ)
        sq_ref[0, rs, :] = (proj(_C_SQ, _C_SK) * (SB_DH ** -0.5)).astype(BF16)
        sk = proj(_C_SK, _C_SV)
        skf_ref[0, :, rs] = sk.T
        skb_ref[0, rs, :] = sk.astype(BF16)
        sv = proj(_C_SV, _C_CB)
        svf_ref[0, :, rs] = sv.T
        svb_ref[0, rs, :] = sv.astype(BF16)
        xb = xb_next
    tail = uc[part - 2:part, :]
    uext_ref[SUBLANES - 2:SUBLANES, :] = tail

    @pl.when(i == pl.num_programs(1) - 1)
    def _():
        cst_ref[0] = tail


def _in_proj(x, pre_ln, ln_g, ln_b, w, wg, bg, cw, cprev):
    B, T, D = x.shape
    tm = min(1024, T)
    n = T // tm
    row = lambda width: pl.BlockSpec((1, tm, width), lambda b, i: (b, i, 0))
    const2 = lambda a: pl.BlockSpec(a.shape, lambda b, i: (0, 0), pipeline_mode=pl.Buffered(1))
    out_shapes = [
        jax.ShapeDtypeStruct((B, T, GLA_KW), F32),
        jax.ShapeDtypeStruct((B, T, GLA_KW), F32),
        jax.ShapeDtypeStruct((B, T, GLA_W), BF16),
        jax.ShapeDtypeStruct((B, T, GLA_W), F32),
        jax.ShapeDtypeStruct((B, T, GLA_KW), F32),
        jax.ShapeDtypeStruct((B, T, SB_W), BF16),
        jax.ShapeDtypeStruct((B, SB_W, T), F32),
        jax.ShapeDtypeStruct((B, SB_W, T), F32),
        jax.ShapeDtypeStruct((B, T, SB_W), BF16),
        jax.ShapeDtypeStruct((B, T, SB_W), BF16),
        jax.ShapeDtypeStruct((B, T, CONV_CH), BF16),
        jax.ShapeDtypeStruct((B, CONV_W - 1, CONV_CH), F32),
    ]
    out_specs = [row(s.shape[-1]) for s in out_shapes[:-1]]
    out_specs[6] = out_specs[7] = pl.BlockSpec((1, SB_W, tm), lambda b, i: (b, 0, i))
    out_specs.append(pl.BlockSpec((1, CONV_W - 1, CONV_CH), lambda b, i: (b, 0, 0)))
    return pl.pallas_call(
        functools.partial(_in_proj_kernel, pre_ln, tm),
        grid=(B, n),
        in_specs=[row(D), const2(ln_g), const2(ln_b), const2(w), const2(wg), const2(bg), const2(cw),
                  pl.BlockSpec((1, CONV_W - 1, CONV_CH), lambda b, i: (b, 0, 0))],
        out_specs=out_specs,
        out_shape=out_shapes,
        scratch_shapes=[pltpu.VMEM((SUBLANES + tm, CONV_CH), F32)],
        compiler_params=pltpu.CompilerParams(dimension_semantics=("arbitrary", "arbitrary"),
                                             vmem_limit_bytes=VMEM_LIMIT_BYTES),
        name="in_proj",
    )(x, ln_g, ln_b, w, wg, bg, cw, cprev)


GLA_LEVELS = (32, 16, 8, 4, 2, 1)
ATT_W = GLA_HEADS * CHUNK
B_PAD = SUBLANES
(_K_NEG_UP, _K_NEG_LO, _K_PAIR) = (0, len(GLA_LEVELS), 2 * len(GLA_LEVELS))
_K_EYE = 3 * len(GLA_LEVELS)
_K_POS4 = _K_EYE + 1
_K_ODD = _K_POS4 + 4
_K_COUNT = _K_ODD + 1


def _gla_constants():
    assert ATT_W == GLA_KW
    t = np.arange(CHUNK)[:, None]
    s = (np.arange(ATT_W) % CHUNK)[None, :]
    ninf = np.float32(-np.inf)
    out = np.zeros((_K_COUNT, CHUNK, ATT_W), np.float32)
    for li, h in enumerate(GLA_LEVELS):
        upper = np.broadcast_to((t & h) != 0, (CHUNK, ATT_W))
        out[_K_NEG_UP + li] = np.where(upper, 0.0, ninf)
        out[_K_NEG_LO + li] = np.where(upper, ninf, 0.0)
        out[_K_PAIR + li] = (((t ^ s) // (2 * h)) == 0) & upper & ((s & h) == 0)
    out[_K_EYE] = t == s
    for p in range(4):
        out[_K_POS4 + p] = np.broadcast_to((t & 3) == p, (CHUNK, ATT_W))
    out[_K_ODD] = np.broadcast_to((t & 1) == 1, (CHUNK, ATT_W))
    tril = (np.arange(CHUNK)[None, :] <= np.arange(CHUNK)[:, None]).astype(np.float32)
    hk = (np.arange(ATT_W)[:, None] // CHUNK == np.arange(GLA_KW)[None, :] // GLA_DK).astype(np.float32)
    hv = (np.arange(ATT_W)[:, None] // CHUNK == np.arange(GLA_W)[None, :] // GLA_DV).astype(np.float32)
    return (jnp.asarray(out), jnp.asarray(tril, BF16), jnp.asarray(hk, BF16), jnp.asarray(hv, BF16))


def _gla_kernel(nc, q_ref, k_ref, v_ref, lg_ref, gr_ref, gn_ref, s0_ref, cst_ref, tril_ref, hk_ref,
                hv_ref, go_ref, sfin_ref, st_ref, b_ref):
    i = pl.program_id(1)
    C = CHUNK
    T = nc * C
    nt = (((1,), (1,)), ((), ()))
    tn = (((0,), (0,)), ((), ()))
    bnn = (((2,), (1,)), ((0,), (0,)))
    bnt = (((2,), (2,)), ((0,), (0,)))

    @pl.when(i == 0)
    def _():
        for h in range(GLA_HEADS):
            st_ref[h] = s0_ref[0, h].T
        b_ref[...] = jnp.zeros(b_ref.shape, F32)

    def chunks(a):
        return a.reshape(nc, C, a.shape[-1])

    q2, k2 = q_ref[0], k_ref[0]
    q, k, v = chunks(q2), chunks(k2), chunks(v_ref[0])
    tril = jnp.broadcast_to(tril_ref[...], (nc, C, C))
    b = sum(lax.dot_general(tril, chunks(p), bnn, preferred_element_type=F32)
            for p in _split_bf16(lg_ref[0], 3))
    b_ref[pl.ds(B_PAD, T), :] = b.reshape(T, GLA_KW)
    b_last = b[:, C - 1:C, :]
    qe = (q * jnp.exp2(b)).astype(BF16)
    kd = (k * jnp.exp2(b_last - b)).astype(BF16)
    e_last = jnp.exp2(b_last)

    def scaled_pair(li, h):
        if h >= SUBLANES:
            zeros = jnp.zeros((nc, h, GLA_KW), F32)
            qs, ks_ = [], []
            for m in range(C // (2 * h)):
                lo = slice(m * 2 * h, m * 2 * h + h)
                up = slice(m * 2 * h + h, (m + 1) * 2 * h)
                r = b[:, m * 2 * h + h - 1:m * 2 * h + h, :]
                qs += [zeros, q[:, up] * jnp.exp2(b[:, up] - r)]
                ks_ += [k[:, lo] * jnp.exp2(r - b[:, lo]), zeros]
            return jnp.concatenate(qs, axis=1).astype(BF16), jnp.concatenate(ks_, axis=1).astype(BF16)
        if h == 4:
            r = jnp.concatenate([jnp.broadcast_to(b[:, m * 8 + 3:m * 8 + 4, :], (nc, 8, GLA_KW))
                                 for m in range(C // 8)], axis=1)
        else:
            bm1 = chunks(b_ref[pl.ds(B_PAD - 1, T), :])
            if h == 1:
                r = b + cst_ref[_K_ODD] * (bm1 - b)
            else:
                bm2 = chunks(b_ref[pl.ds(B_PAD - 2, T), :])
                bp1 = chunks(b_ref[pl.ds(B_PAD + 1, T), :])
                r = ((cst_ref[_K_POS4 + 0] * bp1 + cst_ref[_K_POS4 + 1] * b)
                     + (cst_ref[_K_POS4 + 2] * bm1 + cst_ref[_K_POS4 + 3] * bm2))
        return ((q * jnp.exp2((b - r) + cst_ref[_K_NEG_UP + li])).astype(BF16),
                (k * jnp.exp2((r - b) + cst_ref[_K_NEG_LO + li])).astype(BF16))

    hk = hk_ref[...]
    att = chunks(jnp.dot((q2 * k2).astype(BF16), hk, preferred_element_type=F32)) * cst_ref[_K_EYE]
    for li, h in enumerate(GLA_LEVELS):
        qt, kt = scaled_pair(li, h)
        kbd = jnp.concatenate([kt] * GLA_HEADS, axis=1) * hk
        a = lax.dot_general(qt, kbd, bnt, preferred_element_type=F32)
        att = att + (a if 2 * h == C else a * cst_ref[_K_PAIR + li])
    vbd = jnp.concatenate([v] * GLA_HEADS, axis=1) * hv_ref[...]
    o_intra = lax.dot_general(att.astype(BF16), vbd, bnn, preferred_element_type=F32)

    st = [st_ref[h] for h in range(GLA_HEADS)]
    o_inter = []
    for c in range(nc):
        per_head = []
        for h in range(GLA_HEADS):
            ks = slice(h * GLA_DK, (h + 1) * GLA_DK)
            vs = slice(h * GLA_DV, (h + 1) * GLA_DV)
            per_head.append(lax.dot_general(qe[c][:, ks], st[h].astype(BF16), nt,
                                            preferred_element_type=F32))
            st[h] = st[h] * e_last[c][:, ks] + lax.dot_general(v[c][:, vs], kd[c][:, ks], tn,
                                                               preferred_element_type=F32)
        o_inter.append(jnp.concatenate(per_head, axis=1))
    for h in range(GLA_HEADS):
        st_ref[h] = st[h]
    o = o_intra.reshape(T, GLA_W) + jnp.concatenate(o_inter, axis=0)

    normed = []
    for h in range(GLA_HEADS):
        oh = o[:, h * GLA_DV:(h + 1) * GLA_DV]
        normed.append(oh * lax.rsqrt(jnp.mean(oh * oh, axis=-1, keepdims=True) + LN_EPS))
    gr = gr_ref[0]
    out = jnp.concatenate(normed, axis=1) * gn_ref[...] * (gr * jax.nn.sigmoid(gr))
    go_ref[0] = out.astype(BF16)

    @pl.when(i == pl.num_programs(1) - 1)
    def _():
        for h in range(GLA_HEADS):
            sfin_ref[0, h] = st_ref[h].T


def _gla(gq, gk, gv, lg, gr, gn, s0):
    B, T, _ = gq.shape
    tg = min(1024, T)
    n = T // tg
    row = lambda width: pl.BlockSpec((1, tg, width), lambda b, i: (b, i, 0))
    state = pl.BlockSpec((1, GLA_HEADS, GLA_DK, GLA_DV), lambda b, i: (b, 0, 0, 0))
    const = lambda a: pl.BlockSpec(a.shape, lambda b, i: (0,) * a.ndim, pipeline_mode=pl.Buffered(1))
    consts = _gla_constants()
    return pl.pallas_call(
        functools.partial(_gla_kernel, tg // CHUNK),
        grid=(B, n),
        in_specs=[row(GLA_KW), row(GLA_KW), row(GLA_W), row(GLA_KW), row(GLA_W), const(gn), state]
        + [const(c) for c in consts],
        out_specs=[row(GLA_W), state],
        out_shape=[jax.ShapeDtypeStruct((B, T, GLA_W), BF16),
                   jax.ShapeDtypeStruct((B, GLA_HEADS, GLA_DK, GLA_DV), F32)],
        scratch_shapes=[pltpu.VMEM((GLA_HEADS, GLA_DV, GLA_DK), F32),
                        pltpu.VMEM((tg + 2 * B_PAD, GLA_KW), F32)],
        compiler_params=pltpu.CompilerParams(dimension_semantics=("arbitrary", "arbitrary"),
                                             vmem_limit_bytes=VMEM_LIMIT_BYTES),
        name="gla",
    )(gq, gk, gv, lg, gr, gn, s0, *consts)


SB_WINDOW_PAST = 256


def _suffix_weights():
    j = (np.arange(2 * LANES) % LANES)[:, None]
    s = np.arange(2 * LANES)[None, :]
    return jnp.asarray((j > s) | (s >= LANES), BF16)


SB_MASKED_LOGIT = -1e30


def _sb_step(q_tiles, kws, vws, w, prefix, lims, col_minus_row):
    tq, tw = q_tiles[0].shape[0], kws[0].shape[0]
    n_tiles = len(q_tiles)
    nt = (((1,), (1,)), ((), ()))
    heads = [slice(h * SB_DH, (h + 1) * SB_DH) for h in range(SB_HEADS)]
    rows = SB_HEADS * tq
    st = [dict() for _ in range(n_tiles)]

    def logits(u):
        z = jnp.concatenate([lax.dot_general(q_tiles[u][:, hs], kws[u][:, hs], nt,
                                             preferred_element_type=F32) for hs in heads], axis=0)
        if lims is not None:
            z = jnp.where(col_minus_row < lims[u], z, SB_MASKED_LOGIT)
        st[u]["z"] = z

    def logs(u):
        z = st[u].pop("z")
        log_beta = jnp.minimum(z, 0.0) - jnp.log(1.0 + jnp.exp(-jnp.abs(z)))
        st[u]["log_beta"] = log_beta
        st[u]["keep"] = _split_bf16(log_beta - z, 2)

    def suffix(u):
        hi, lo = st[u].pop("keep")
        before = None if prefix is None else prefix[u * rows:(u + 1) * rows]
        after = [None] * (tw // LANES)
        total = None
        for blk in reversed(range(tw // LANES)):
            cs = slice(blk * LANES, (blk + 1) * LANES)
            res = jnp.dot(jnp.concatenate([hi[:, cs], lo[:, cs]], axis=1), w, preferred_element_type=F32)
            later = total if before is None else (before if total is None else before + total)
            after[blk] = res[:, :LANES] if later is None else res[:, :LANES] + later
            total = res[:, LANES:] if total is None else total + res[:, LANES:]
        st[u]["after"] = after
        st[u]["total"] = total

    def weights(u):
        after = st[u].pop("after")
        after = after[0] if len(after) == 1 else jnp.concatenate(after, axis=1)
        st[u]["a"] = jnp.exp(st[u].pop("log_beta") + after).astype(BF16)

    def values(u):
        a = st[u].pop("a")
        st[u]["out"] = [jnp.dot(a[h * tq:(h + 1) * tq], vws[u][:, hs], preferred_element_type=F32)
                        for h, hs in enumerate(heads)]

    stages = (logits, logs, suffix, weights, values)
    for step in range(n_tiles + len(stages) - 1):
        for depth in reversed(range(len(stages))):
            if 0 <= step - depth < n_tiles:
                stages[depth](step - depth)
    totals = [s["total"] for s in st]
    return [s["out"] for s in st], totals[0] if n_tiles == 1 else jnp.concatenate(totals, axis=0)


def _sb_kernel(tq, nsub, tw, wp, n_past, layer, past_is_self, *refs):
    if past_is_self:
        q_ref, kp_ref, vp_ref, w_ref, cmr_ref, o_ref = refs
        kp_ref = kp_ref.at[0, 0]
        vp_ref = vp_ref.at[0, 0]
    else:
        (q_ref, kd_ref, vd_ref, kwin_ref, vwin_ref, kc_hbm, vc_hbm, w_ref, cmr_ref, o_ref,
         kbuf_ref, vbuf_ref, sem) = refs
    i = pl.program_id(1)
    q_tiles = [q_ref[0, u * tq:(u + 1) * tq, :] for u in range(nsub)]

    if past_is_self:
        q0s = [(i * nsub + u) * tq for u in range(nsub)]
        rests = [pl.multiple_of(jnp.maximum(q0 - wp, 0), SB_KBLOCK) for q0 in q0s]
        kws = [kp_ref[pl.ds(r, tw), :] for r in rests]
        vws = [vp_ref[pl.ds(r, tw), :] for r in rests]
        lims = [q0 - r for q0, r in zip(q0s, rests)]
    else:
        rests = [n_past - wp]
        pad = jnp.zeros((tw - wp - tq, SB_W), BF16)
        kws = [jnp.concatenate([kwin_ref[0, 0].T.astype(BF16), kd_ref[0], pad], axis=0)]
        vws = [jnp.concatenate([vwin_ref[0, 0].T.astype(BF16), vd_ref[0], pad], axis=0)]
        lims = [wp]

    def past_block(start):
        if past_is_self:
            return kp_ref[pl.ds(start, SB_KBLOCK), :], vp_ref[pl.ds(start, SB_KBLOCK), :]
        b = pl.program_id(0)
        copies = [pltpu.make_async_copy(src.at[layer, b, :, pl.ds(start, SB_KBLOCK)], dst, sem.at[n])
                  for n, (src, dst) in enumerate(((kc_hbm, kbuf_ref), (vc_hbm, vbuf_ref)))]
        for cp in copies:
            cp.start()
        for cp in copies:
            cp.wait()
        return kbuf_ref[...].T.astype(BF16), vbuf_ref[...].T.astype(BF16)
    outs, prefs = _sb_step(q_tiles, kws, vws, w_ref[...], None, lims, cmr_ref[...])

    def store(u, heads_out):
        o_ref[0, u * tq:(u + 1) * tq, :] = jnp.concatenate(heads_out, axis=1).astype(BF16)

    rows = SB_HEADS * tq
    nblks = [r // SB_KBLOCK if isinstance(r, int) else lax.shift_right_logical(r, 7) for r in rests]
    alive = []
    for u in range(nsub):
        store(u, outs[u])
        alive.append(jnp.logical_and(nblks[u] > 0, jnp.max(prefs[u * rows:(u + 1) * rows]) > SB_DEAD_LOG))

    @pl.when(functools.reduce(jnp.logical_or, alive))
    def _():
        for u in range(nsub):
            def cond(carry, nblk=nblks[u]):
                return jnp.logical_and(carry[0] < nblk, carry[1])

            def body(carry, rest=rests[u], u=u):
                j, pref = carry[0], carry[2]
                start = pl.multiple_of(rest - (j + 1) * SB_KBLOCK, SB_KBLOCK)
                kb, vb = past_block(start)
                o_new, p_new = _sb_step([q_tiles[u]], [kb], [vb], w_ref[...], pref, None, None)
                pref = pref + p_new
                return (j + 1, jnp.max(pref) > SB_DEAD_LOG, pref,
                        *[o + n for o, n in zip(carry[3:], o_new[0])])

            final = lax.while_loop(cond, body, (jnp.int32(0), alive[u], prefs[u * rows:(u + 1) * rows],
                                                *outs[u]))
            store(u, final[3:])


def _sb(sq, k_new, v_new, k_past, v_past, layer, past_is_self):
    B, T, _ = sq.shape
    P = k_past.shape[2] if past_is_self else k_past.shape[3]
    tq = min(SB_KBLOCK, T)
    nsub = 8 if past_is_self and T % (8 * tq) == 0 else 1
    n = T // (nsub * tq)
    wp = min(SB_WINDOW_PAST, P - tq if past_is_self else P)
    tw = -(-(wp + tq) // LANES) * LANES
    assert T % tq == 0 and tq & (tq - 1) == 0 and P % SB_KBLOCK == 0 and wp % SB_KBLOCK == 0
    assert T == tq or tq == SB_KBLOCK
    assert not past_is_self or tw == wp + tq
    row = pl.BlockSpec((1, nsub * tq, SB_W), lambda b, i: (b, i, 0))
    const = lambda a: pl.BlockSpec(a.shape, lambda b, i: (0, 0), pipeline_mode=pl.Buffered(1))
    w = _suffix_weights()
    cmr = jnp.asarray(np.arange(tw)[None, :] - (np.arange(SB_HEADS * tq) % tq)[:, None], jnp.int32)
    if past_is_self:
        full = pl.BlockSpec((1, 1, P, SB_W), lambda b, i: (0, b, 0, 0), pipeline_mode=pl.Buffered(1))
        args = [sq, k_past, v_past, w, cmr]
        specs = [row, full, full, const(w), const(cmr)]
        scratch = []
    else:
        assert P % wp == 0
        win = pl.BlockSpec((1, 1, SB_W, wp), lambda b, i: (layer, b, 0, P // wp - 1))
        hbm = pl.BlockSpec(memory_space=pl.ANY)
        args = [sq, k_new, v_new, k_past, v_past, k_past, v_past, w, cmr]
        specs = [row, row, row, win, win, hbm, hbm, const(w), const(cmr)]
        scratch = [pltpu.VMEM((SB_W, SB_KBLOCK), F32), pltpu.VMEM((SB_W, SB_KBLOCK), F32),
                   pltpu.SemaphoreType.DMA((2,))]
    return pl.pallas_call(
        functools.partial(_sb_kernel, tq, nsub, tw, wp, P, layer, past_is_self),
        grid=(B, n),
        in_specs=specs,
        out_specs=row,
        out_shape=jax.ShapeDtypeStruct((B, T, SB_W), BF16),
        scratch_shapes=scratch,
        compiler_params=pltpu.CompilerParams(dimension_semantics=("arbitrary", "arbitrary"),
                                             vmem_limit_bytes=VMEM_LIMIT_BYTES),
        name="sb",
    )(*args)


FF_CHUNK = 1024
POST_PART_ROWS = 256


def _post_kernel(pre_ln, x_ref, go_ref, so_ref, co_ref, lng_ref, lnb_ref, wo_ref, g1_ref, b1_ref,
                 wu_ref, wd_ref, g2_ref, b2_ref, y_ref):
    tm = x_ref.shape[0]
    n_ff = D_FF // FF_CHUNK
    if tm % POST_PART_ROWS == 0 and POST_PART_ROWS % (n_ff * SUBLANES) == 0:
        halves = [(r, POST_PART_ROWS) for r in range(0, tm, POST_PART_ROWS)]
    else:
        halves = [(0, tm)]

    def mix_rows(r0, n):
        rs = slice(r0, r0 + n)
        return (jnp.dot(go_ref[rs, :], wo_ref[0:GLA_W, :], preferred_element_type=F32)
                + jnp.dot(so_ref[rs, :], wo_ref[GLA_W:GLA_W + SB_W, :], preferred_element_type=F32)
                + jnp.dot(co_ref[rs, :], wo_ref[GLA_W + SB_W:MIX_W, :], preferred_element_type=F32))

    def ln1_rows(mix, r0, n, off):
        x = x_ref[r0 + off:r0 + off + n, :]
        if pre_ln:
            x = _layer_norm(x, lng_ref[...], lnb_ref[...])
        return _layer_norm(DEEPNORM_ALPHA * x + mix[off:off + n], g1_ref[...], b1_ref[...])

    def ffn_chunk(x1b, c):
        c0 = c * FF_CHUNK
        h = jnp.dot(x1b, wu_ref[:, c0:c0 + FF_CHUNK], preferred_element_type=F32)
        h = jnp.square(jnp.maximum(h, 0.0)).astype(BF16)
        return jnp.dot(h, wd_ref[c0:c0 + FF_CHUNK, :], preferred_element_type=F32)

    def ln2_rows(x1, ff, r0, n, off):
        y_ref[r0 + off:r0 + off + n, :] = _layer_norm(DEEPNORM_ALPHA * x1[off:off + n] + ff[off:off + n],
                                                     g2_ref[...], b2_ref[...])

    r0, n = halves[0]
    x1 = ln1_rows(mix_rows(r0, n), r0, n, 0)
    prev = None
    for hi in range(len(halves)):
        r0, n = halves[hi]
        nxt = halves[hi + 1] if hi + 1 < len(halves) else None
        nxt_mix = mix_rows(*nxt) if nxt else None
        x1b = x1.astype(BF16)
        g = n // n_ff
        ff, nxt_x1 = None, []
        for c in range(n_ff):
            part = ffn_chunk(x1b, c)
            ff = part if ff is None else ff + part
            if prev is not None:
                ln2_rows(prev[0], prev[1], prev[2], g, c * g)
            if nxt:
                nxt_x1.append(ln1_rows(nxt_mix, nxt[0], g, c * g))
        prev = (x1, ff, r0, n)
        if nxt:
            x1 = jnp.concatenate(nxt_x1, axis=0)
    ln2_rows(prev[0], prev[1], prev[2], prev[3], 0)


def _post(x, go, so, co, pre_ln, ln_g, ln_b, wo, g1, b1, wu, wd, g2, b2):
    N, D = x.shape
    tm = min(1024, N)
    row = lambda width: pl.BlockSpec((tm, width), lambda i: (i, 0))
    const = lambda a: pl.BlockSpec(a.shape, lambda i: (0, 0), pipeline_mode=pl.Buffered(1))
    return pl.pallas_call(
        functools.partial(_post_kernel, pre_ln),
        grid=(N // tm,),
        in_specs=[row(D), row(GLA_W), row(SB_W), row(CONV_CH), const(ln_g), const(ln_b), const(wo),
                  const(g1), const(b1), const(wu), const(wd), const(g2), const(b2)],
        out_specs=row(D),
        out_shape=jax.ShapeDtypeStruct((N, D), F32),
        compiler_params=pltpu.CompilerParams(dimension_semantics=("arbitrary",),
                                             vmem_limit_bytes=VMEM_LIMIT_BYTES),
        name="post",
    )(x, go, so, co, ln_g, ln_b, wo, g1, b1, wu, wd, g2, b2)


def _prep_layer(l, w_in, w_gate_up, b_gate, gla_norm_g, conv_w, w_out, ln1_g, ln1_b, w_up, w_down,
                ln2_g, ln2_b):
    wl = w_in[l]
    w = jnp.concatenate([wl[:, :_GLOW_SRC], wl[:, _GLOW_SRC + GLA_GATE_RANK:],
                         wl[:, _GLOW_SRC:_GLOW_SRC + GLA_GATE_RANK],
                         jnp.zeros((D_MODEL, LANES - GLA_GATE_RANK), F32)], axis=1).astype(BF16)
    wg = jnp.concatenate([w_gate_up[l], jnp.zeros((LANES - GLA_GATE_RANK, GLA_KW), F32)], axis=0).astype(BF16)
    r2 = lambda a: a.reshape(1, -1)
    return dict(w=w, wg=wg, bg=r2(b_gate[l]), gn=r2(gla_norm_g[l]), cw=conv_w[l],
                wo=w_out[l].astype(BF16), g1=r2(ln1_g[l]), b1=r2(ln1_b[l]),
                wu=w_up[l].astype(BF16), wd=w_down[l].astype(BF16), g2=r2(ln2_g[l]), b2=r2(ln2_b[l]))


def _trunk(x, c_k, c_v, s_gla, s_conv, ln_g, ln_b, layers):
    B, T, D = x.shape
    ks, vs, gs, cs = [], [], [], []
    for l, p in enumerate(layers):
        pre_ln = l == 0
        if c_k is None:
            s0 = jnp.zeros((B, GLA_HEADS, GLA_DK, GLA_DV), F32)
            cprev = jnp.zeros((B, CONV_W - 1, CONV_CH), F32)
        else:
            s0, cprev = s_gla[l], s_conv[l]
        gq, gk, gv, gr, lg, sq, skf, svf, skb, svb, co, cst = _in_proj(
            x, pre_ln, ln_g, ln_b, p["w"], p["wg"], p["bg"], p["cw"], cprev)
        go, s_fin = _gla(gq, gk, gv, lg, gr, p["gn"], s0)
        if c_k is None:
            so = _sb(sq, skb, svb, skb[None], svb[None], 0, True)
        else:
            key_minor = lambda c: jnp.transpose(c, (0, 1, 3, 4, 2)).reshape(DEPTH, B, SB_W, -1)
            so = _sb(sq, skb, svb, key_minor(c_k), key_minor(c_v), l, False)
        x = _post(x.reshape(B * T, D), go.reshape(B * T, GLA_W), so.reshape(B * T, SB_W),
                  co.reshape(B * T, CONV_CH), pre_ln, ln_g, ln_b, p["wo"], p["g1"], p["b1"],
                  p["wu"], p["wd"], p["g2"], p["b2"]).reshape(B, T, D)
        ks.append(skf)
        vs.append(svf)
        gs.append(s_fin)
        cs.append(cst)
    token_major = lambda a: jnp.transpose(jnp.stack(a).reshape(DEPTH, B, SB_HEADS, SB_DH, T), (0, 1, 4, 2, 3))
    return x, token_major(ks), token_major(vs), jnp.stack(gs), jnp.stack(cs)


def kernel(x_prompt, x_sample, cache_sb_k, cache_sb_v, state_gla, state_conv, ln_in_g, ln_in_b, w_in,
           w_gate_up, b_gate, gla_norm_g, conv_w, w_out, ln1_g, ln1_b, w_up, w_down, ln2_g, ln2_b):
    layers = [_prep_layer(l, w_in, w_gate_up, b_gate, gla_norm_g, conv_w, w_out, ln1_g, ln1_b, w_up,
                          w_down, ln2_g, ln2_b) for l in range(DEPTH)]
    ln_g, ln_b = ln_in_g.reshape(1, -1), ln_in_b.reshape(1, -1)
    y_p, k_p, v_p, gla_p, conv_p = _trunk(x_prompt, None, None, None, None, ln_g, ln_b, layers)
    y_s, k_s, v_s, gla_s, conv_s = _trunk(x_sample, cache_sb_k, cache_sb_v, state_gla, state_conv,
                                          ln_g, ln_b, layers)
    return (y_p, y_s, k_p, v_p, gla_p, conv_p, k_s, v_s, gla_s, conv_s)
```

```python
import functools

import numpy as np

import jax
import jax.numpy as jnp
from jax import lax
from jax.experimental import pallas as pl
from jax.experimental.pallas import tpu as pltpu

F32 = jnp.float32
BF16 = jnp.bfloat16

D_MODEL = 1024
DEPTH = 2
CHUNK = 64
GLA_HEADS = 4
GLA_DK = 64
GLA_DV = 128
GLA_GATE_RANK = 16
GLA_TAU = 16.0
GLA_W = GLA_HEADS * GLA_DV
GLA_KW = GLA_HEADS * GLA_DK
SB_HEADS = 4
SB_DH = 64
SB_W = SB_HEADS * SB_DH
CONV_CH = 256
CONV_W = 3
MIX_W = GLA_W + SB_W + CONV_CH
D_FF = 4 * D_MODEL
LN_EPS = 1e-5
DEEPNORM_ALPHA = (2 * DEPTH) ** 0.25
LOG2_E = 1.4426950408889634

LANES = 128
SUBLANES = 8
SB_KBLOCK = 128
SB_DEAD_LOG = -104.0
VMEM_LIMIT_BYTES = 56 * 1024 * 1024

_C_GQ = 0
_C_GK = _C_GQ + GLA_KW
_C_GV = _C_GK + GLA_KW
_C_GR = _C_GV + GLA_W
_C_SQ = _C_GR + GLA_W
_C_SK = _C_SQ + SB_W
_C_SV = _C_SK + SB_W
_C_CB = _C_SV + SB_W
_C_CC = _C_CB + CONV_CH
_C_CH = _C_CC + CONV_CH
_C_GLOW = _C_CH + CONV_CH
N_IN_PAD = _C_GLOW + LANES
_GLOW_SRC = 2 * GLA_KW + 2 * GLA_W


def _layer_norm(x, g, b):
    mu = jnp.mean(x, axis=-1, keepdims=True)
    xc = x - mu
    var = jnp.mean(xc * xc, axis=-1, keepdims=True)
    return xc * lax.rsqrt(var + LN_EPS) * g + b


def _softplus_neg_abs(z):
    return jnp.log1p(jnp.exp(-jnp.abs(z)))


def _split_bf16(x, n):
    parts = []
    r = x
    for _ in range(n - 1):
        p = r.astype(BF16)
        parts.append(p)
        r = r - p.astype(F32)
    parts.append(r.astype(BF16))
    return parts


IN_PROJ_PART_ROWS = 256


def _in_proj_kernel(pre_ln, tm, kv_slabs, n_aliased, *refs):
    x_ref, lng_ref, lnb_ref, w_ref, wg_ref, bg_ref, cw_ref, cprev_ref = refs[:8]
    (gq_ref, gk_ref, gv_ref, gr_ref, lg_ref, sq_ref, skf_ref, svf_ref, skb_ref, svb_ref,
     co_ref, cst_ref, uext_ref) = refs[8 + n_aliased:]
    i = pl.program_id(1)

    @pl.when(i == 0)
    def _():
        uext_ref[SUBLANES - 2:SUBLANES, :] = cprev_ref[0]

    part = IN_PROJ_PART_ROWS if tm % IN_PROJ_PART_ROWS == 0 else tm

    def normed_rows(r0):
        x = x_ref[0, r0:r0 + part, :]
        if pre_ln:
            x = _layer_norm(x, lng_ref[...], lnb_ref[...])
        return x.astype(BF16)

    cw = cw_ref[...]
    xb = normed_rows(0)
    for r0 in range(0, tm, part):
        rs = slice(r0, r0 + part)

        def proj(a, b, xb=xb):
            return jnp.dot(xb, w_ref[:, a:b], preferred_element_type=F32)

        glow = proj(_C_GLOW, N_IN_PAD)
        u = jnp.dot(glow.astype(BF16), wg_ref[...], preferred_element_type=F32) + bg_ref[...]
        lg_ref[0, rs, :] = (jnp.minimum(u, 0.0) - _softplus_neg_abs(u)) * (LOG2_E / GLA_TAU)

        cb = proj(_C_CB, _C_CC)
        uc = proj(_C_CC, _C_CH) * proj(_C_CH, _C_GLOW)
        uext_ref[SUBLANES + r0:SUBLANES + r0 + part, :] = uc
        um1 = uext_ref[SUBLANES - 1 + r0:SUBLANES - 1 + r0 + part, :]
        um2 = uext_ref[SUBLANES - 2 + r0:SUBLANES - 2 + r0 + part, :]
        cy = (cw[2:3, :] * uc + cw[0:1, :] * um2) + cw[1:2, :] * um1
        co_ref[0, rs, :] = (cb * cy).astype(BF16)

        gq_ref[0, rs, :] = proj(_C_GQ, _C_GK) * (GLA_DK ** -0.5)
        gk_ref[0, rs, :] = proj(_C_GK, _C_GV)
        gv_ref[0, rs, :] = proj(_C_GV, _C_GR).astype(BF16)
        xb_next = normed_rows(r0 + part) if r0 + part < tm else None
        gr_ref[0, rs, :] = proj(_C_GR, _C_SQ)
        sq_ref[0, rs, :] = (proj(_C_SQ, _C_SK) * (SB_DH ** -0.5)).astype(BF16)
        for c0, c1, f_ref, b_ref in ((_C_SK, _C_SV, skf_ref, skb_ref), (_C_SV, _C_CB, svf_ref, svb_ref)):
            kv = proj(c0, c1)
            b_ref[0, rs, :] = kv.astype(BF16)
            if kv_slabs == 0:
                f_ref[0, rs, :] = kv
            else:
                kv_t = kv.T
                for slab in range(kv_slabs):
                    f_ref[slab, 0, :, rs] = kv_t
        xb = xb_next
    tail = uc[part - 2:part, :]
    uext_ref[SUBLANES - 2:SUBLANES, :] = tail

    @pl.when(i == pl.num_programs(1) - 1)
    def _():
        cst_ref[0] = tail


def _in_proj(x, pre_ln, ln_g, ln_b, w, wg, bg, cw, cprev, kv_layer=None, kv_prev=None):
    B, T, D = x.shape
    tm = min(1024, T)
    n = T // tm
    row = lambda width: pl.BlockSpec((1, tm, width), lambda b, i: (b, i, 0))
    const2 = lambda a: pl.BlockSpec(a.shape, lambda b, i: (0, 0), pipeline_mode=pl.Buffered(1))
    stacked = kv_layer is not None
    assert (kv_prev is None) == (not stacked or kv_layer == 0)
    aliased = [] if kv_prev is None else list(kv_prev)
    kv_slabs = 0 if not stacked else (DEPTH if kv_layer == 0 else 1)
    kv_shape = (DEPTH, B, SB_W, T) if stacked else (B, T, SB_W)
    out_shapes = [
        jax.ShapeDtypeStruct((B, T, GLA_KW), F32),
        jax.ShapeDtypeStruct((B, T, GLA_KW), F32),
        jax.ShapeDtypeStruct((B, T, GLA_W), BF16),
        jax.ShapeDtypeStruct((B, T, GLA_W), F32),
        jax.ShapeDtypeStruct((B, T, GLA_KW), F32),
        jax.ShapeDtypeStruct((B, T, SB_W), BF16),
        jax.ShapeDtypeStruct(kv_shape, F32),
        jax.ShapeDtypeStruct(kv_shape, F32),
        jax.ShapeDtypeStruct((B, T, SB_W), BF16),
        jax.ShapeDtypeStruct((B, T, SB_W), BF16),
        jax.ShapeDtypeStruct((B, T, CONV_CH), BF16),
        jax.ShapeDtypeStruct((B, CONV_W - 1, CONV_CH), F32),
    ]
    out_specs = [row(s.shape[-1]) for s in out_shapes[:-1]]
    if stacked:
        out_specs[6] = out_specs[7] = pl.BlockSpec((kv_slabs, 1, SB_W, tm), lambda b, i: (kv_layer, b, 0, i))
    out_specs.append(pl.BlockSpec((1, CONV_W - 1, CONV_CH), lambda b, i: (b, 0, 0)))
    n_in = 8
    return pl.pallas_call(
        functools.partial(_in_proj_kernel, pre_ln, tm, kv_slabs, len(aliased)),
        grid=(B, n),
        in_specs=[row(D), const2(ln_g), const2(ln_b), const2(w), const2(wg), const2(bg), const2(cw),
                  pl.BlockSpec((1, CONV_W - 1, CONV_CH), lambda b, i: (b, 0, 0))]
        + [pl.BlockSpec(memory_space=pl.ANY)] * len(aliased),
        out_specs=out_specs,
        out_shape=out_shapes,
        input_output_aliases={n_in + j: 6 + j for j in range(len(aliased))},
        scratch_shapes=[pltpu.VMEM((SUBLANES + tm, CONV_CH), F32)],
        compiler_params=pltpu.CompilerParams(dimension_semantics=("arbitrary", "arbitrary"),
                                             vmem_limit_bytes=VMEM_LIMIT_BYTES),
        name="in_proj",
    )(x, ln_g, ln_b, w, wg, bg, cw, cprev, *aliased)


GLA_LEVELS = (32, 16, 8, 4, 2, 1)
ATT_W = GLA_HEADS * CHUNK
B_PAD = SUBLANES
(_K_NEG_UP, _K_NEG_LO, _K_PAIR) = (0, len(GLA_LEVELS), 2 * len(GLA_LEVELS))
_K_EYE = 3 * len(GLA_LEVELS)
_K_POS4 = _K_EYE + 1
_K_ODD = _K_POS4 + 4
_K_COUNT = _K_ODD + 1


def _gla_constants():
    assert ATT_W == GLA_KW
    t = np.arange(CHUNK)[:, None]
    s = (np.arange(ATT_W) % CHUNK)[None, :]
    ninf = np.float32(-np.inf)
    out = np.zeros((_K_COUNT, CHUNK, ATT_W), np.float32)
    for li, h in enumerate(GLA_LEVELS):
        upper = np.broadcast_to((t & h) != 0, (CHUNK, ATT_W))
        out[_K_NEG_UP + li] = np.where(upper, 0.0, ninf)
        out[_K_NEG_LO + li] = np.where(upper, ninf, 0.0)
        out[_K_PAIR + li] = (((t ^ s) // (2 * h)) == 0) & upper & ((s & h) == 0)
    out[_K_EYE] = t == s
    for p in range(4):
        out[_K_POS4 + p] = np.broadcast_to((t & 3) == p, (CHUNK, ATT_W))
    out[_K_ODD] = np.broadcast_to((t & 1) == 1, (CHUNK, ATT_W))
    tril = (np.arange(CHUNK)[None, :] <= np.arange(CHUNK)[:, None]).astype(np.float32)
    hk = (np.arange(ATT_W)[:, None] // CHUNK == np.arange(GLA_KW)[None, :] // GLA_DK).astype(np.float32)
    hv = (np.arange(ATT_W)[:, None] // CHUNK == np.arange(GLA_W)[None, :] // GLA_DV).astype(np.float32)
    return (jnp.asarray(out), jnp.asarray(tril, BF16), jnp.asarray(hk, BF16), jnp.asarray(hv, BF16))


def _gla_kernel(nc, q_ref, k_ref, v_ref, lg_ref, gr_ref, gn_ref, s0_ref, cst_ref, tril_ref, hk_ref,
                hv_ref, go_ref, sfin_ref, st_ref, b_ref):
    i = pl.program_id(1)
    C = CHUNK
    T = nc * C
    nt = (((1,), (1,)), ((), ()))
    tn = (((0,), (0,)), ((), ()))
    bnn = (((2,), (1,)), ((0,), (0,)))
    bnt = (((2,), (2,)), ((0,), (0,)))

    @pl.when(i == 0)
    def _():
        for h in range(GLA_HEADS):
            st_ref[h] = s0_ref[0, h].T
        b_ref[...] = jnp.zeros(b_ref.shape, F32)

    def chunks(a):
        return a.reshape(nc, C, a.shape[-1])

    q2, k2 = q_ref[0], k_ref[0]
    q, k, v = chunks(q2), chunks(k2), chunks(v_ref[0])
    tril = jnp.broadcast_to(tril_ref[...], (nc, C, C))
    b = sum(lax.dot_general(tril, chunks(p), bnn, preferred_element_type=F32)
            for p in _split_bf16(lg_ref[0], 3))
    b_ref[pl.ds(B_PAD, T), :] = b.reshape(T, GLA_KW)
    b_last = b[:, C - 1:C, :]
    qe = (q * jnp.exp2(b)).astype(BF16)
    kd = (k * jnp.exp2(b_last - b)).astype(BF16)
    e_last = jnp.exp2(b_last)

    def scaled_pair(li, h):
        if h >= SUBLANES:
            zeros = jnp.zeros((nc, h, GLA_KW), F32)
            qs, ks_ = [], []
            for m in range(C // (2 * h)):
                lo = slice(m * 2 * h, m * 2 * h + h)
                up = slice(m * 2 * h + h, (m + 1) * 2 * h)
                r = b[:, m * 2 * h + h - 1:m * 2 * h + h, :]
                qs += [zeros, q[:, up] * jnp.exp2(b[:, up] - r)]
                ks_ += [k[:, lo] * jnp.exp2(r - b[:, lo]), zeros]
            return jnp.concatenate(qs, axis=1).astype(BF16), jnp.concatenate(ks_, axis=1).astype(BF16)
        if h == 4:
            r = jnp.concatenate([jnp.broadcast_to(b[:, m * 8 + 3:m * 8 + 4, :], (nc, 8, GLA_KW))
                                 for m in range(C // 8)], axis=1)
        else:
            bm1 = chunks(b_ref[pl.ds(B_PAD - 1, T), :])
            if h == 1:
                r = b + cst_ref[_K_ODD] * (bm1 - b)
            else:
                bm2 = chunks(b_ref[pl.ds(B_PAD - 2, T), :])
                bp1 = chunks(b_ref[pl.ds(B_PAD + 1, T), :])
                r = ((cst_ref[_K_POS4 + 0] * bp1 + cst_ref[_K_POS4 + 1] * b)
                     + (cst_ref[_K_POS4 + 2] * bm1 + cst_ref[_K_POS4 + 3] * bm2))
        return ((q * jnp.exp2((b - r) + cst_ref[_K_NEG_UP + li])).astype(BF16),
                (k * jnp.exp2((r - b) + cst_ref[_K_NEG_LO + li])).astype(BF16))

    hk = hk_ref[...]
    att = chunks(jnp.dot((q2 * k2).astype(BF16), hk, preferred_element_type=F32)) * cst_ref[_K_EYE]
    for li, h in enumerate(GLA_LEVELS):
        qt, kt = scaled_pair(li, h)
        kbd = jnp.concatenate([kt] * GLA_HEADS, axis=1) * hk
        a = lax.dot_general(qt, kbd, bnt, preferred_element_type=F32)
        att = att + (a if 2 * h == C else a * cst_ref[_K_PAIR + li])
    vbd = jnp.concatenate([v] * GLA_HEADS, axis=1) * hv_ref[...]
    o_intra = lax.dot_general(att.astype(BF16), vbd, bnn, preferred_element_type=F32)

    st = [st_ref[h] for h in range(GLA_HEADS)]
    o_inter = []
    for c in range(nc):
        per_head = []
        for h in range(GLA_HEADS):
            ks = slice(h * GLA_DK, (h + 1) * GLA_DK)
            vs = slice(h * GLA_DV, (h + 1) * GLA_DV)
            per_head.append(lax.dot_general(qe[c][:, ks], st[h].astype(BF16), nt,
                                            preferred_element_type=F32))
            st[h] = st[h] * e_last[c][:, ks] + lax.dot_general(v[c][:, vs], kd[c][:, ks], tn,
                                                               preferred_element_type=F32)
        o_inter.append(jnp.concatenate(per_head, axis=1))
    for h in range(GLA_HEADS):
        st_ref[h] = st[h]
    o = o_intra.reshape(T, GLA_W) + jnp.concatenate(o_inter, axis=0)

    normed = []
    for h in range(GLA_HEADS):
        oh = o[:, h * GLA_DV:(h + 1) * GLA_DV]
        normed.append(oh * lax.rsqrt(jnp.mean(oh * oh, axis=-1, keepdims=True) + LN_EPS))
    gr = gr_ref[0]
    out = jnp.concatenate(normed, axis=1) * gn_ref[...] * (gr * jax.nn.sigmoid(gr))
    go_ref[0] = out.astype(BF16)

    @pl.when(i == pl.num_programs(1) - 1)
    def _():
        for h in range(GLA_HEADS):
            sfin_ref[0, h] = st_ref[h].T


def _gla(gq, gk, gv, lg, gr, gn, s0):
    B, T, _ = gq.shape
    tg = min(2048, T)
    n = T // tg
    row = lambda width: pl.BlockSpec((1, tg, width), lambda b, i: (b, i, 0))
    state = pl.BlockSpec((1, GLA_HEADS, GLA_DK, GLA_DV), lambda b, i: (b, 0, 0, 0))
    const = lambda a: pl.BlockSpec(a.shape, lambda b, i: (0,) * a.ndim, pipeline_mode=pl.Buffered(1))
    consts = _gla_constants()
    return pl.pallas_call(
        functools.partial(_gla_kernel, tg // CHUNK),
        grid=(B, n),
        in_specs=[row(GLA_KW), row(GLA_KW), row(GLA_W), row(GLA_KW), row(GLA_W), const(gn), state]
        + [const(c) for c in consts],
        out_specs=[row(GLA_W), state],
        out_shape=[jax.ShapeDtypeStruct((B, T, GLA_W), BF16),
                   jax.ShapeDtypeStruct((B, GLA_HEADS, GLA_DK, GLA_DV), F32)],
        scratch_shapes=[pltpu.VMEM((GLA_HEADS, GLA_DV, GLA_DK), F32),
                        pltpu.VMEM((tg + 2 * B_PAD, GLA_KW), F32)],
        compiler_params=pltpu.CompilerParams(dimension_semantics=("arbitrary", "arbitrary"),
                                             vmem_limit_bytes=VMEM_LIMIT_BYTES),
        name="gla",
    )(gq, gk, gv, lg, gr, gn, s0, *consts)


SB_WINDOW_PAST = 256


def _suffix_weights():
    j = (np.arange(2 * LANES) % LANES)[:, None]
    s = np.arange(2 * LANES)[None, :]
    return jnp.asarray((j > s) | (s >= LANES), BF16)


SB_MASKED_LOGIT = -1e30


def _sb_step(q_tiles, kws, vws, w, prefix, lims, col_minus_row):
    tq, tw = q_tiles[0].shape[0], kws[0].shape[0]
    n_tiles = len(q_tiles)
    nt = (((1,), (1,)), ((), ()))
    heads = [slice(h * SB_DH, (h + 1) * SB_DH) for h in range(SB_HEADS)]
    rows = SB_HEADS * tq
    st = [dict() for _ in range(n_tiles)]

    def logits(u):
        z = jnp.concatenate([lax.dot_general(q_tiles[u][:, hs], kws[u][:, hs], nt,
                                             preferred_element_type=F32) for hs in heads], axis=0)
        if lims is not None:
            z = jnp.where(col_minus_row < lims[u], z, SB_MASKED_LOGIT)
        st[u]["z"] = z

    def logs(u):
        z = st[u].pop("z")
        log_beta = jnp.minimum(z, 0.0) - jnp.log(1.0 + jnp.exp(-jnp.abs(z)))
        st[u]["log_beta"] = log_beta
        st[u]["keep"] = _split_bf16(log_beta - z, 2)

    def suffix(u):
        hi, lo = st[u].pop("keep")
        before = None if prefix is None else prefix[u * rows:(u + 1) * rows]
        after = [None] * (tw // LANES)
        total = None
        for blk in reversed(range(tw // LANES)):
            cs = slice(blk * LANES, (blk + 1) * LANES)
            res = jnp.dot(jnp.concatenate([hi[:, cs], lo[:, cs]], axis=1), w, preferred_element_type=F32)
            later = total if before is None else (before if total is None else before + total)
            after[blk] = res[:, :LANES] if later is None else res[:, :LANES] + later
            total = res[:, LANES:] if total is None else total + res[:, LANES:]
        st[u]["after"] = after
        st[u]["total"] = total

    def weights(u):
        after = st[u].pop("after")
        after = after[0] if len(after) == 1 else jnp.concatenate(after, axis=1)
        st[u]["a"] = jnp.exp(st[u].pop("log_beta") + after).astype(BF16)

    def values(u):
        a = st[u].pop("a")
        st[u]["out"] = [jnp.dot(a[h * tq:(h + 1) * tq], vws[u][:, hs], preferred_element_type=F32)
                        for h, hs in enumerate(heads)]

    stages = (logits, logs, suffix, weights, values)
    for step in range(n_tiles + len(stages) - 1):
        for depth in reversed(range(len(stages))):
            if 0 <= step - depth < n_tiles:
                stages[depth](step - depth)
    totals = [s["total"] for s in st]
    return [s["out"] for s in st], totals[0] if n_tiles == 1 else jnp.concatenate(totals, axis=0)


def _sb_kernel(tq, nsub, tw, wp, n_past, layer, past_is_self, *refs):
    if past_is_self:
        q_ref, kp_ref, vp_ref, w_ref, cmr_ref, o_ref = refs
        kp_ref = kp_ref.at[0, 0]
        vp_ref = vp_ref.at[0, 0]
    else:
        (q_ref, kd_ref, vd_ref, kwin_ref, vwin_ref, kc_hbm, vc_hbm, w_ref, cmr_ref, o_ref,
         kbuf_ref, vbuf_ref, sem) = refs
    i = pl.program_id(1)
    q_tiles = [q_ref[0, u * tq:(u + 1) * tq, :] for u in range(nsub)]

    if past_is_self:
        q0s = [(i * nsub + u) * tq for u in range(nsub)]
        rests = [pl.multiple_of(jnp.maximum(q0 - wp, 0), SB_KBLOCK) for q0 in q0s]
        kws = [kp_ref[pl.ds(r, tw), :] for r in rests]
        vws = [vp_ref[pl.ds(r, tw), :] for r in rests]
        lims = [q0 - r for q0, r in zip(q0s, rests)]
    else:
        rests = [n_past - wp]
        pad = jnp.zeros((tw - wp - tq, SB_W), BF16)
        kws = [jnp.concatenate([kwin_ref[0, 0].T.astype(BF16), kd_ref[0], pad], axis=0)]
        vws = [jnp.concatenate([vwin_ref[0, 0].T.astype(BF16), vd_ref[0], pad], axis=0)]
        lims = [wp]

    def past_block(start):
        if past_is_self:
            return kp_ref[pl.ds(start, SB_KBLOCK), :], vp_ref[pl.ds(start, SB_KBLOCK), :]
        b = pl.program_id(0)
        copies = [pltpu.make_async_copy(src.at[layer, b, :, pl.ds(start, SB_KBLOCK)], dst, sem.at[n])
                  for n, (src, dst) in enumerate(((kc_hbm, kbuf_ref), (vc_hbm, vbuf_ref)))]
        for cp in copies:
            cp.start()
        for cp in copies:
            cp.wait()
        return kbuf_ref[...].T.astype(BF16), vbuf_ref[...].T.astype(BF16)
    outs, prefs = _sb_step(q_tiles, kws, vws, w_ref[...], None, lims, cmr_ref[...])

    def store(u, heads_out):
        o_ref[0, u * tq:(u + 1) * tq, :] = jnp.concatenate(heads_out, axis=1).astype(BF16)

    rows = SB_HEADS * tq
    nblks = [r // SB_KBLOCK if isinstance(r, int) else lax.shift_right_logical(r, 7) for r in rests]
    alive = []
    for u in range(nsub):
        store(u, outs[u])
        alive.append(jnp.logical_and(nblks[u] > 0, jnp.max(prefs[u * rows:(u + 1) * rows]) > SB_DEAD_LOG))

    @pl.when(functools.reduce(jnp.logical_or, alive))
    def _():
        for u in range(nsub):
            def cond(carry, nblk=nblks[u]):
                return jnp.logical_and(carry[0] < nblk, carry[1])

            def body(carry, rest=rests[u], u=u):
                j, pref = carry[0], carry[2]
                start = pl.multiple_of(rest - (j + 1) * SB_KBLOCK, SB_KBLOCK)
                kb, vb = past_block(start)
                o_new, p_new = _sb_step([q_tiles[u]], [kb], [vb], w_ref[...], pref, None, None)
                pref = pref + p_new
                return (j + 1, jnp.max(pref) > SB_DEAD_LOG, pref,
                        *[o + n for o, n in zip(carry[3:], o_new[0])])

            final = lax.while_loop(cond, body, (jnp.int32(0), alive[u], prefs[u * rows:(u + 1) * rows],
                                                *outs[u]))
            store(u, final[3:])


def _sb(sq, k_new, v_new, k_past, v_past, layer, past_is_self):
    B, T, _ = sq.shape
    P = k_past.shape[2] if past_is_self else k_past.shape[3]
    tq = min(SB_KBLOCK, T)
    nsub = 8 if past_is_self and T % (8 * tq) == 0 else 1
    n = T // (nsub * tq)
    wp = min(SB_WINDOW_PAST, P - tq if past_is_self else P)
    tw = -(-(wp + tq) // LANES) * LANES
    assert T % tq == 0 and tq & (tq - 1) == 0 and P % SB_KBLOCK == 0 and wp % SB_KBLOCK == 0
    assert T == tq or tq == SB_KBLOCK
    assert not past_is_self or tw == wp + tq
    row = pl.BlockSpec((1, nsub * tq, SB_W), lambda b, i: (b, i, 0))
    const = lambda a: pl.BlockSpec(a.shape, lambda b, i: (0, 0), pipeline_mode=pl.Buffered(1))
    w = _suffix_weights()
    cmr = jnp.asarray(np.arange(tw)[None, :] - (np.arange(SB_HEADS * tq) % tq)[:, None], jnp.int32)
    if past_is_self:
        full = pl.BlockSpec((1, 1, P, SB_W), lambda b, i: (0, b, 0, 0), pipeline_mode=pl.Buffered(1))
        args = [sq, k_past, v_past, w, cmr]
        specs = [row, full, full, const(w), const(cmr)]
        scratch = []
    else:
        assert P % wp == 0
        win = pl.BlockSpec((1, 1, SB_W, wp), lambda b, i: (layer, b, 0, P // wp - 1))
        hbm = pl.BlockSpec(memory_space=pl.ANY)
        args = [sq, k_new, v_new, k_past, v_past, k_past, v_past, w, cmr]
        specs = [row, row, row, win, win, hbm, hbm, const(w), const(cmr)]
        scratch = [pltpu.VMEM((SB_W, SB_KBLOCK), F32), pltpu.VMEM((SB_W, SB_KBLOCK), F32),
                   pltpu.SemaphoreType.DMA((2,))]
    return pl.pallas_call(
        functools.partial(_sb_kernel, tq, nsub, tw, wp, P, layer, past_is_self),
        grid=(B, n),
        in_specs=specs,
        out_specs=row,
        out_shape=jax.ShapeDtypeStruct((B, T, SB_W), BF16),
        scratch_shapes=scratch,
        compiler_params=pltpu.CompilerParams(dimension_semantics=("arbitrary", "arbitrary"),
                                             vmem_limit_bytes=VMEM_LIMIT_BYTES),
        name="sb",
    )(*args)


FF_CHUNK = 1024
POST_PART_ROWS = 256


def _post_kernel(pre_ln, x_ref, go_ref, so_ref, co_ref, lng_ref, lnb_ref, wo_ref, g1_ref, b1_ref,
                 wu_ref, wd_ref, g2_ref, b2_ref, y_ref):
    tm = x_ref.shape[0]
    n_ff = D_FF // FF_CHUNK
    if tm % POST_PART_ROWS == 0 and POST_PART_ROWS % (n_ff * SUBLANES) == 0:
        halves = [(r, POST_PART_ROWS) for r in range(0, tm, POST_PART_ROWS)]
    else:
        halves = [(0, tm)]

    def mix_rows(r0, n):
        rs = slice(r0, r0 + n)
        return (jnp.dot(go_ref[rs, :], wo_ref[0:GLA_W, :], preferred_element_type=F32)
                + jnp.dot(so_ref[rs, :], wo_ref[GLA_W:GLA_W + SB_W, :], preferred_element_type=F32)
                + jnp.dot(co_ref[rs, :], wo_ref[GLA_W + SB_W:MIX_W, :], preferred_element_type=F32))

    def ln1_rows(mix, r0, n, off):
        x = x_ref[r0 + off:r0 + off + n, :]
        if pre_ln:
            x = _layer_norm(x, lng_ref[...], lnb_ref[...])
        return _layer_norm(DEEPNORM_ALPHA * x + mix[off:off + n], g1_ref[...], b1_ref[...])

    def ffn_chunk(x1b, c):
        c0 = c * FF_CHUNK
        h = jnp.dot(x1b, wu_ref[:, c0:c0 + FF_CHUNK], preferred_element_type=F32)
        h = jnp.square(jnp.maximum(h, 0.0)).astype(BF16)
        return jnp.dot(h, wd_ref[c0:c0 + FF_CHUNK, :], preferred_element_type=F32)

    def ln2_rows(x1, ff, r0, n, off):
        y_ref[r0 + off:r0 + off + n, :] = _layer_norm(DEEPNORM_ALPHA * x1[off:off + n] + ff[off:off + n],
                                                     g2_ref[...], b2_ref[...])

    r0, n = halves[0]
    x1 = ln1_rows(mix_rows(r0, n), r0, n, 0)
    prev = None
    for hi in range(len(halves)):
        r0, n = halves[hi]
        nxt = halves[hi + 1] if hi + 1 < len(halves) else None
        nxt_mix = mix_rows(*nxt) if nxt else None
        x1b = x1.astype(BF16)
        g = n // n_ff
        ff, nxt_x1 = None, []
        for c in range(n_ff):
            part = ffn_chunk(x1b, c)
            ff = part if ff is None else ff + part
            if prev is not None:
                ln2_rows(prev[0], prev[1], prev[2], g, c * g)
            if nxt:
                nxt_x1.append(ln1_rows(nxt_mix, nxt[0], g, c * g))
        prev = (x1, ff, r0, n)
        if nxt:
            x1 = jnp.concatenate(nxt_x1, axis=0)
    ln2_rows(prev[0], prev[1], prev[2], prev[3], 0)


def _post(x, go, so, co, pre_ln, ln_g, ln_b, wo, g1, b1, wu, wd, g2, b2):
    N, D = x.shape
    tm = min(1024, N)
    row = lambda width: pl.BlockSpec((tm, width), lambda i: (i, 0))
    const = lambda a: pl.BlockSpec(a.shape, lambda i: (0, 0), pipeline_mode=pl.Buffered(1))
    return pl.pallas_call(
        functools.partial(_post_kernel, pre_ln),
        grid=(N // tm,),
        in_specs=[row(D), row(GLA_W), row(SB_W), row(CONV_CH), const(ln_g), const(ln_b), const(wo),
                  const(g1), const(b1), const(wu), const(wd), const(g2), const(b2)],
        out_specs=row(D),
        out_shape=jax.ShapeDtypeStruct((N, D), F32),
        compiler_params=pltpu.CompilerParams(dimension_semantics=("arbitrary",),
                                             vmem_limit_bytes=VMEM_LIMIT_BYTES),
        name="post",
    )(x, go, so, co, ln_g, ln_b, wo, g1, b1, wu, wd, g2, b2)


def _prep_layer(l, w_in, w_gate_up, b_gate, gla_norm_g, conv_w, w_out, ln1_g, ln1_b, w_up, w_down,
                ln2_g, ln2_b):
    wl = w_in[l]
    w = jnp.concatenate([wl[:, :_GLOW_SRC], wl[:, _GLOW_SRC + GLA_GATE_RANK:],
                         wl[:, _GLOW_SRC:_GLOW_SRC + GLA_GATE_RANK],
                         jnp.zeros((D_MODEL, LANES - GLA_GATE_RANK), F32)], axis=1).astype(BF16)
    wg = jnp.concatenate([w_gate_up[l], jnp.zeros((LANES - GLA_GATE_RANK, GLA_KW), F32)], axis=0).astype(BF16)
    r2 = lambda a: a.reshape(1, -1)
    return dict(w=w, wg=wg, bg=r2(b_gate[l]), gn=r2(gla_norm_g[l]), cw=conv_w[l],
                wo=w_out[l].astype(BF16), g1=r2(ln1_g[l]), b1=r2(ln1_b[l]),
                wu=w_up[l].astype(BF16), wd=w_down[l].astype(BF16), g2=r2(ln2_g[l]), b2=r2(ln2_b[l]))


def _trunk(x, c_k, c_v, s_gla, s_conv, ln_g, ln_b, layers):
    B, T, D = x.shape
    ks, vs, gs, cs = [], [], [], []
    kv_all = None
    for l, p in enumerate(layers):
        pre_ln = l == 0
        if c_k is None:
            s0 = jnp.zeros((B, GLA_HEADS, GLA_DK, GLA_DV), F32)
            cprev = jnp.zeros((B, CONV_W - 1, CONV_CH), F32)
            kv_args = dict(kv_layer=l, kv_prev=kv_all)
        else:
            s0, cprev = s_gla[l], s_conv[l]
            kv_args = {}
        gq, gk, gv, gr, lg, sq, skf, svf, skb, svb, co, cst = _in_proj(
            x, pre_ln, ln_g, ln_b, p["w"], p["wg"], p["bg"], p["cw"], cprev, **kv_args)
        if c_k is None:
            kv_all = (skf, svf)
        go, s_fin = _gla(gq, gk, gv, lg, gr, p["gn"], s0)
        if c_k is None:
            so = _sb(sq, skb, svb, skb[None], svb[None], 0, True)
        else:
            key_minor = lambda c: jnp.transpose(c, (0, 1, 3, 4, 2)).reshape(DEPTH, B, SB_W, -1)
            so = _sb(sq, skb, svb, key_minor(c_k), key_minor(c_v), l, False)
        x = _post(x.reshape(B * T, D), go.reshape(B * T, GLA_W), so.reshape(B * T, SB_W),
                  co.reshape(B * T, CONV_CH), pre_ln, ln_g, ln_b, p["wo"], p["g1"], p["b1"],
                  p["wu"], p["wd"], p["g2"], p["b2"]).reshape(B, T, D)
        ks.append(skf)
        vs.append(svf)
        gs.append(s_fin)
        cs.append(cst)
    if kv_all is not None:
        k_out, v_out = (jnp.transpose(a.reshape(DEPTH, B, SB_HEADS, SB_DH, T), (0, 1, 4, 2, 3)) for a in kv_all)
    else:
        k_out, v_out = (jnp.stack(a).reshape(DEPTH, B, T, SB_HEADS, SB_DH) for a in (ks, vs))
    return x, k_out, v_out, jnp.stack(gs), jnp.stack(cs)


def kernel(x_prompt, x_sample, cache_sb_k, cache_sb_v, state_gla, state_conv, ln_in_g, ln_in_b, w_in,
           w_gate_up, b_gate, gla_norm_g, conv_w, w_out, ln1_g, ln1_b, w_up, w_down, ln2_g, ln2_b):
    layers = [_prep_layer(l, w_in, w_gate_up, b_gate, gla_norm_g, conv_w, w_out, ln1_g, ln1_b, w_up,
                          w_down, ln2_g, ln2_b) for l in range(DEPTH)]
    ln_g, ln_b = ln_in_g.reshape(1, -1), ln_in_b.reshape(1, -1)
    y_p, k_p, v_p, gla_p, conv_p = _trunk(x_prompt, None, None, None, None, ln_g, ln_b, layers)
    y_s, k_s, v_s, gla_s, conv_s = _trunk(x_sample, cache_sb_k, cache_sb_v, state_gla, state_conv,
                                          ln_g, ln_b, layers)
    return (y_p, y_s, k_p, v_p, gla_p, conv_p, k_s, v_s, gla_s, conv_s)
```

```python
import functools

import numpy as np

import jax
import jax.numpy as jnp
from jax import lax
from jax.experimental import pallas as pl
from jax.experimental.pallas import tpu as pltpu

F32 = jnp.float32
BF16 = jnp.bfloat16

D_MODEL = 1024
DEPTH = 2
CHUNK = 64
GLA_HEADS = 4
GLA_DK = 64
GLA_DV = 128
GLA_GATE_RANK = 16
GLA_TAU = 16.0
GLA_W = GLA_HEADS * GLA_DV
GLA_KW = GLA_HEADS * GLA_DK
SB_HEADS = 4
SB_DH = 64
SB_W = SB_HEADS * SB_DH
CONV_CH = 256
CONV_W = 3
MIX_W = GLA_W + SB_W + CONV_CH
D_FF = 4 * D_MODEL
LN_EPS = 1e-5
DEEPNORM_ALPHA = (2 * DEPTH) ** 0.25
LOG2_E = 1.4426950408889634

LANES = 128
SUBLANES = 8
SB_KBLOCK = 128
SB_DEAD_LOG = -104.0
VMEM_LIMIT_BYTES = 56 * 1024 * 1024
DENSE_TILE_ROWS = 1024
GLA_TILE_ROWS = 1024
SB_TILES_PER_STEP = 8

_C_GQ = 0
_C_GK = _C_GQ + GLA_KW
_C_GV = _C_GK + GLA_KW
_C_GR = _C_GV + GLA_W
_C_SQ = _C_GR + GLA_W
_C_SK = _C_SQ + SB_W
_C_SV = _C_SK + SB_W
_C_CB = _C_SV + SB_W
_C_CC = _C_CB + CONV_CH
_C_CH = _C_CC + CONV_CH
_C_GLOW = _C_CH + CONV_CH
N_IN_PAD = _C_GLOW + LANES
_GLOW_SRC = 2 * GLA_KW + 2 * GLA_W


def _layer_norm(x, g, b):
    mu = jnp.mean(x, axis=-1, keepdims=True)
    xc = x - mu
    var = jnp.mean(xc * xc, axis=-1, keepdims=True)
    return xc * lax.rsqrt(var + LN_EPS) * g + b


def _softplus_neg_abs(z):
    return jnp.log1p(jnp.exp(-jnp.abs(z)))


def _split_bf16(x, n):
    parts = []
    r = x
    for _ in range(n - 1):
        p = r.astype(BF16)
        parts.append(p)
        r = r - p.astype(F32)
    parts.append(r.astype(BF16))
    return parts


IN_PROJ_PART_ROWS = 256


def _in_proj_kernel(pre_ln, tm, kv_slabs, n_aliased, *refs):
    x_ref, lng_ref, lnb_ref, w_ref, wg_ref, bg_ref, cw_ref, cprev_ref = refs[:8]
    (gq_ref, gk_ref, gv_ref, gr_ref, lg_ref, sq_ref, skf_ref, svf_ref, skb_ref, svb_ref,
     co_ref, cst_ref, uext_ref) = refs[8 + n_aliased:]
    i = pl.program_id(1)

    @pl.when(i == 0)
    def _():
        uext_ref[SUBLANES - 2:SUBLANES, :] = cprev_ref[0]

    part = IN_PROJ_PART_ROWS if tm % IN_PROJ_PART_ROWS == 0 else tm

    def normed_rows(r0):
        x = x_ref[0, r0:r0 + part, :]
        if pre_ln:
            x = _layer_norm(x, lng_ref[...], lnb_ref[...])
        return x.astype(BF16)

    cw = cw_ref[...]
    xb = normed_rows(0)
    for r0 in range(0, tm, part):
        rs = slice(r0, r0 + part)

        def proj(a, b, xb=xb):
            return jnp.dot(xb, w_ref[:, a:b], preferred_element_type=F32)

        glow = proj(_C_GLOW, N_IN_PAD)
        u = jnp.dot(glow.astype(BF16), wg_ref[...], preferred_element_type=F32) + bg_ref[...]
        lg_ref[0, rs, :] = (jnp.minimum(u, 0.0) - _softplus_neg_abs(u)) * (LOG2_E / GLA_TAU)

        cb = proj(_C_CB, _C_CC)
        uc = proj(_C_CC, _C_CH) * proj(_C_CH, _C_GLOW)
        uext_ref[SUBLANES + r0:SUBLANES + r0 + part, :] = uc
        um1 = uext_ref[SUBLANES - 1 + r0:SUBLANES - 1 + r0 + part, :]
        um2 = uext_ref[SUBLANES - 2 + r0:SUBLANES - 2 + r0 + part, :]
        cy = (cw[2:3, :] * uc + cw[0:1, :] * um2) + cw[1:2, :] * um1
        co_ref[0, rs, :] = (cb * cy).astype(BF16)

        gq_ref[0, rs, :] = proj(_C_GQ, _C_GK) * (GLA_DK ** -0.5)
        gk_ref[0, rs, :] = proj(_C_GK, _C_GV)
        gv_ref[0, rs, :] = proj(_C_GV, _C_GR).astype(BF16)
        xb_next = normed_rows(r0 + part) if r0 + part < tm else None
        gr_ref[0, rs, :] = proj(_C_GR, _C_SQ)
        sq_ref[0, rs, :] = (proj(_C_SQ, _C_SK) * (SB_DH ** -0.5)).astype(BF16)
        for c0, c1, f_ref, b_ref in ((_C_SK, _C_SV, skf_ref, skb_ref), (_C_SV, _C_CB, svf_ref, svb_ref)):
            kv = proj(c0, c1)
            b_ref[0, rs, :] = kv.astype(BF16)
            if kv_slabs == 0:
                f_ref[0, rs, :] = kv
            else:
                kv_t = kv.T
                for slab in range(kv_slabs):
                    f_ref[slab, 0, :, rs] = kv_t
        xb = xb_next
    tail = uc[part - 2:part, :]
    uext_ref[SUBLANES - 2:SUBLANES, :] = tail

    @pl.when(i == pl.num_programs(1) - 1)
    def _():
        cst_ref[0] = tail


def _in_proj(x, pre_ln, ln_g, ln_b, w, wg, bg, cw, cprev, kv_layer=None, kv_prev=None):
    B, T, D = x.shape
    tm = min(DENSE_TILE_ROWS, T)
    n = T // tm
    row = lambda width: pl.BlockSpec((1, tm, width), lambda b, i: (b, i, 0))
    const2 = lambda a: pl.BlockSpec(a.shape, lambda b, i: (0, 0), pipeline_mode=pl.Buffered(1))
    stacked = kv_layer is not None
    assert (kv_prev is None) == (not stacked or kv_layer == 0)
    aliased = [] if kv_prev is None else list(kv_prev)
    kv_slabs = 0 if not stacked else (DEPTH if kv_layer == 0 else 1)
    kv_shape = (DEPTH, B, SB_W, T) if stacked else (B, T, SB_W)
    out_shapes = [
        jax.ShapeDtypeStruct((B, T, GLA_KW), F32),
        jax.ShapeDtypeStruct((B, T, GLA_KW), F32),
        jax.ShapeDtypeStruct((B, T, GLA_W), BF16),
        jax.ShapeDtypeStruct((B, T, GLA_W), F32),
        jax.ShapeDtypeStruct((B, T, GLA_KW), F32),
        jax.ShapeDtypeStruct((B, T, SB_W), BF16),
        jax.ShapeDtypeStruct(kv_shape, F32),
        jax.ShapeDtypeStruct(kv_shape, F32),
        jax.ShapeDtypeStruct((B, T, SB_W), BF16),
        jax.ShapeDtypeStruct((B, T, SB_W), BF16),
        jax.ShapeDtypeStruct((B, T, CONV_CH), BF16),
        jax.ShapeDtypeStruct((B, CONV_W - 1, CONV_CH), F32),
    ]
    out_specs = [row(s.shape[-1]) for s in out_shapes[:-1]]
    if stacked:
        out_specs[6] = out_specs[7] = pl.BlockSpec((kv_slabs, 1, SB_W, tm), lambda b, i: (kv_layer, b, 0, i))
    out_specs.append(pl.BlockSpec((1, CONV_W - 1, CONV_CH), lambda b, i: (b, 0, 0)))
    n_in = 8
    return pl.pallas_call(
        functools.partial(_in_proj_kernel, pre_ln, tm, kv_slabs, len(aliased)),
        grid=(B, n),
        in_specs=[row(D), const2(ln_g), const2(ln_b), const2(w), const2(wg), const2(bg), const2(cw),
                  pl.BlockSpec((1, CONV_W - 1, CONV_CH), lambda b, i: (b, 0, 0))]
        + [pl.BlockSpec(memory_space=pl.ANY)] * len(aliased),
        out_specs=out_specs,
        out_shape=out_shapes,
        input_output_aliases={n_in + j: 6 + j for j in range(len(aliased))},
        scratch_shapes=[pltpu.VMEM((SUBLANES + tm, CONV_CH), F32)],
        compiler_params=pltpu.CompilerParams(dimension_semantics=("arbitrary", "arbitrary"),
                                             vmem_limit_bytes=VMEM_LIMIT_BYTES),
        name="in_proj",
    )(x, ln_g, ln_b, w, wg, bg, cw, cprev, *aliased)


GLA_LEVELS = (32, 16, 8, 4, 2, 1)
ATT_W = GLA_HEADS * CHUNK
B_PAD = SUBLANES
(_K_NEG_UP, _K_NEG_LO, _K_PAIR) = (0, len(GLA_LEVELS), 2 * len(GLA_LEVELS))
_K_EYE = 3 * len(GLA_LEVELS)
_K_POS4 = _K_EYE + 1
_K_ODD = _K_POS4 + 4
_K_COUNT = _K_ODD + 1


def _gla_constants():
    assert ATT_W == GLA_KW
    t = np.arange(CHUNK)[:, None]
    s = (np.arange(ATT_W) % CHUNK)[None, :]
    ninf = np.float32(-np.inf)
    out = np.zeros((_K_COUNT, CHUNK, ATT_W), np.float32)
    for li, h in enumerate(GLA_LEVELS):
        upper = np.broadcast_to((t & h) != 0, (CHUNK, ATT_W))
        out[_K_NEG_UP + li] = np.where(upper, 0.0, ninf)
        out[_K_NEG_LO + li] = np.where(upper, ninf, 0.0)
        out[_K_PAIR + li] = (((t ^ s) // (2 * h)) == 0) & upper & ((s & h) == 0)
    out[_K_EYE] = t == s
    for p in range(4):
        out[_K_POS4 + p] = np.broadcast_to((t & 3) == p, (CHUNK, ATT_W))
    out[_K_ODD] = np.broadcast_to((t & 1) == 1, (CHUNK, ATT_W))
    tril = (np.arange(CHUNK)[None, :] <= np.arange(CHUNK)[:, None]).astype(np.float32)
    hk = (np.arange(ATT_W)[:, None] // CHUNK == np.arange(GLA_KW)[None, :] // GLA_DK).astype(np.float32)
    hv = (np.arange(ATT_W)[:, None] // CHUNK == np.arange(GLA_W)[None, :] // GLA_DV).astype(np.float32)
    return (jnp.asarray(out), jnp.asarray(tril, BF16), jnp.asarray(hk, BF16), jnp.asarray(hv, BF16))


def _gla_kernel(nc, q_ref, k_ref, v_ref, lg_ref, gr_ref, gn_ref, s0_ref, cst_ref, tril_ref, hk_ref,
                hv_ref, go_ref, sfin_ref, st_ref, b_ref):
    i = pl.program_id(1)
    C = CHUNK
    T = nc * C
    nt = (((1,), (1,)), ((), ()))
    tn = (((0,), (0,)), ((), ()))
    bnn = (((2,), (1,)), ((0,), (0,)))
    bnt = (((2,), (2,)), ((0,), (0,)))

    @pl.when(i == 0)
    def _():
        for h in range(GLA_HEADS):
            st_ref[h] = s0_ref[0, h].T
        b_ref[...] = jnp.zeros(b_ref.shape, F32)

    def chunks(a):
        return a.reshape(nc, C, a.shape[-1])

    q2, k2 = q_ref[0], k_ref[0]
    q, k, v = chunks(q2), chunks(k2), chunks(v_ref[0])
    tril = jnp.broadcast_to(tril_ref[...], (nc, C, C))
    b = sum(lax.dot_general(tril, chunks(p), bnn, preferred_element_type=F32)
            for p in _split_bf16(lg_ref[0], 3))
    b_ref[pl.ds(B_PAD, T), :] = b.reshape(T, GLA_KW)
    b_last = b[:, C - 1:C, :]
    qe = (q * jnp.exp2(b)).astype(BF16)
    kd = (k * jnp.exp2(b_last - b)).astype(BF16)
    e_last = jnp.exp2(b_last)

    def scaled_pair(li, h):
        if h >= SUBLANES:
            zeros = jnp.zeros((nc, h, GLA_KW), F32)
            qs, ks_ = [], []
            for m in range(C // (2 * h)):
                lo = slice(m * 2 * h, m * 2 * h + h)
                up = slice(m * 2 * h + h, (m + 1) * 2 * h)
                r = b[:, m * 2 * h + h - 1:m * 2 * h + h, :]
                qs += [zeros, q[:, up] * jnp.exp2(b[:, up] - r)]
                ks_ += [k[:, lo] * jnp.exp2(r - b[:, lo]), zeros]
            return jnp.concatenate(qs, axis=1).astype(BF16), jnp.concatenate(ks_, axis=1).astype(BF16)
        if h == 4:
            r = jnp.concatenate([jnp.broadcast_to(b[:, m * 8 + 3:m * 8 + 4, :], (nc, 8, GLA_KW))
                                 for m in range(C // 8)], axis=1)
        else:
            bm1 = chunks(b_ref[pl.ds(B_PAD - 1, T), :])
            if h == 1:
                r = b + cst_ref[_K_ODD] * (bm1 - b)
            else:
                bm2 = chunks(b_ref[pl.ds(B_PAD - 2, T), :])
                bp1 = chunks(b_ref[pl.ds(B_PAD + 1, T), :])
                r = ((cst_ref[_K_POS4 + 0] * bp1 + cst_ref[_K_POS4 + 1] * b)
                     + (cst_ref[_K_POS4 + 2] * bm1 + cst_ref[_K_POS4 + 3] * bm2))
        return ((q * jnp.exp2((b - r) + cst_ref[_K_NEG_UP + li])).astype(BF16),
                (k * jnp.exp2((r - b) + cst_ref[_K_NEG_LO + li])).astype(BF16))

    hk = hk_ref[...]
    att = chunks(jnp.dot((q2 * k2).astype(BF16), hk, preferred_element_type=F32)) * cst_ref[_K_EYE]
    for li, h in enumerate(GLA_LEVELS):
        qt, kt = scaled_pair(li, h)
        kbd = jnp.concatenate([kt] * GLA_HEADS, axis=1) * hk
        a = lax.dot_general(qt, kbd, bnt, preferred_element_type=F32)
        att = att + (a if 2 * h == C else a * cst_ref[_K_PAIR + li])
    vbd = jnp.concatenate([v] * GLA_HEADS, axis=1) * hv_ref[...]
    o_intra = lax.dot_general(att.astype(BF16), vbd, bnn, preferred_element_type=F32)

    st = [st_ref[h] for h in range(GLA_HEADS)]
    o_inter = []
    for c in range(nc):
        per_head = []
        for h in range(GLA_HEADS):
            ks = slice(h * GLA_DK, (h + 1) * GLA_DK)
            vs = slice(h * GLA_DV, (h + 1) * GLA_DV)
            per_head.append(lax.dot_general(qe[c][:, ks], st[h].astype(BF16), nt,
                                            preferred_element_type=F32))
            st[h] = st[h] * e_last[c][:, ks] + lax.dot_general(v[c][:, vs], kd[c][:, ks], tn,
                                                               preferred_element_type=F32)
        o_inter.append(jnp.concatenate(per_head, axis=1))
    for h in range(GLA_HEADS):
        st_ref[h] = st[h]
    o = o_intra.reshape(T, GLA_W) + jnp.concatenate(o_inter, axis=0)

    normed = []
    for h in range(GLA_HEADS):
        oh = o[:, h * GLA_DV:(h + 1) * GLA_DV]
        normed.append(oh * lax.rsqrt(jnp.mean(oh * oh, axis=-1, keepdims=True) + LN_EPS))
    gr = gr_ref[0]
    out = jnp.concatenate(normed, axis=1) * gn_ref[...] * (gr * jax.nn.sigmoid(gr))
    go_ref[0] = out.astype(BF16)

    @pl.when(i == pl.num_programs(1) - 1)
    def _():
        for h in range(GLA_HEADS):
            sfin_ref[0, h] = st_ref[h].T


def _gla(gq, gk, gv, lg, gr, gn, s0):
    B, T, _ = gq.shape
    tg = min(GLA_TILE_ROWS, T)
    n = T // tg
    row = lambda width: pl.BlockSpec((1, tg, width), lambda b, i: (b, i, 0))
    state = pl.BlockSpec((1, GLA_HEADS, GLA_DK, GLA_DV), lambda b, i: (b, 0, 0, 0))
    const = lambda a: pl.BlockSpec(a.shape, lambda b, i: (0,) * a.ndim, pipeline_mode=pl.Buffered(1))
    consts = _gla_constants()
    return pl.pallas_call(
        functools.partial(_gla_kernel, tg // CHUNK),
        grid=(B, n),
        in_specs=[row(GLA_KW), row(GLA_KW), row(GLA_W), row(GLA_KW), row(GLA_W), const(gn), state]
        + [const(c) for c in consts],
        out_specs=[row(GLA_W), state],
        out_shape=[jax.ShapeDtypeStruct((B, T, GLA_W), BF16),
                   jax.ShapeDtypeStruct((B, GLA_HEADS, GLA_DK, GLA_DV), F32)],
        scratch_shapes=[pltpu.VMEM((GLA_HEADS, GLA_DV, GLA_DK), F32),
                        pltpu.VMEM((tg + 2 * B_PAD, GLA_KW), F32)],
        compiler_params=pltpu.CompilerParams(dimension_semantics=("arbitrary", "arbitrary"),
                                             vmem_limit_bytes=VMEM_LIMIT_BYTES),
        name="gla",
    )(gq, gk, gv, lg, gr, gn, s0, *consts)


SB_WINDOW_PAST = 256


def _suffix_weights():
    j = (np.arange(2 * LANES) % LANES)[:, None]
    s = np.arange(2 * LANES)[None, :]
    return jnp.asarray((j > s) | (s >= LANES), BF16)


SB_MASKED_LOGIT = -1e30


def _sb_step(q_tiles, kws, vws, w, prefix, lims, col_minus_row):
    tq, tw = q_tiles[0].shape[0], kws[0].shape[0]
    n_tiles = len(q_tiles)
    nt = (((1,), (1,)), ((), ()))
    heads = [slice(h * SB_DH, (h + 1) * SB_DH) for h in range(SB_HEADS)]
    rows = SB_HEADS * tq
    st = [dict() for _ in range(n_tiles)]

    def logits(u):
        z = jnp.concatenate([lax.dot_general(q_tiles[u][:, hs], kws[u][:, hs], nt,
                                             preferred_element_type=F32) for hs in heads], axis=0)
        if lims is not None:
            z = jnp.where(col_minus_row < lims[u], z, SB_MASKED_LOGIT)
        st[u]["z"] = z

    def logs(u):
        z = st[u].pop("z")
        log_beta = jnp.minimum(z, 0.0) - jnp.log(1.0 + jnp.exp(-jnp.abs(z)))
        st[u]["log_beta"] = log_beta
        st[u]["keep"] = _split_bf16(log_beta - z, 2)

    def suffix(u):
        hi, lo = st[u].pop("keep")
        before = None if prefix is None else prefix[u * rows:(u + 1) * rows]
        after = [None] * (tw // LANES)
        total = None
        for blk in reversed(range(tw // LANES)):
            cs = slice(blk * LANES, (blk + 1) * LANES)
            res = jnp.dot(jnp.concatenate([hi[:, cs], lo[:, cs]], axis=1), w, preferred_element_type=F32)
            later = total if before is None else (before if total is None else before + total)
            after[blk] = res[:, :LANES] if later is None else res[:, :LANES] + later
            total = res[:, LANES:] if total is None else total + res[:, LANES:]
        st[u]["after"] = after
        st[u]["total"] = total

    def weights(u):
        after = st[u].pop("after")
        after = after[0] if len(after) == 1 else jnp.concatenate(after, axis=1)
        st[u]["a"] = jnp.exp(st[u].pop("log_beta") + after).astype(BF16)

    def values(u):
        a = st[u].pop("a")
        st[u]["out"] = [jnp.dot(a[h * tq:(h + 1) * tq], vws[u][:, hs], preferred_element_type=F32)
                        for h, hs in enumerate(heads)]

    stages = (logits, logs, suffix, weights, values)
    for step in range(n_tiles + len(stages) - 1):
        for depth in reversed(range(len(stages))):
            if 0 <= step - depth < n_tiles:
                stages[depth](step - depth)
    totals = [s["total"] for s in st]
    return [s["out"] for s in st], totals[0] if n_tiles == 1 else jnp.concatenate(totals, axis=0)


def _sb_kernel(tq, nsub, tw, wp, n_past, layer, past_is_self, *refs):
    if past_is_self:
        q_ref, kp_ref, vp_ref, w_ref, cmr_ref, o_ref = refs
        kp_ref = kp_ref.at[0, 0]
        vp_ref = vp_ref.at[0, 0]
    else:
        (q_ref, kd_ref, vd_ref, kwin_ref, vwin_ref, kc_hbm, vc_hbm, w_ref, cmr_ref, o_ref,
         kbuf_ref, vbuf_ref, sem) = refs
    i = pl.program_id(1)
    q_tiles = [q_ref[0, u * tq:(u + 1) * tq, :] for u in range(nsub)]

    if past_is_self:
        q0s = [(i * nsub + u) * tq for u in range(nsub)]
        rests = [pl.multiple_of(jnp.maximum(q0 - wp, 0), SB_KBLOCK) for q0 in q0s]
        kws = [kp_ref[pl.ds(r, tw), :] for r in rests]
        vws = [vp_ref[pl.ds(r, tw), :] for r in rests]
        lims = [q0 - r for q0, r in zip(q0s, rests)]
    else:
        rests = [n_past - wp]
        pad = jnp.zeros((tw - wp - tq, SB_W), BF16)
        kws = [jnp.concatenate([kwin_ref[0, 0].T.astype(BF16), kd_ref[0], pad], axis=0)]
        vws = [jnp.concatenate([vwin_ref[0, 0].T.astype(BF16), vd_ref[0], pad], axis=0)]
        lims = [wp]

    def past_block(start):
        if past_is_self:
            return kp_ref[pl.ds(start, SB_KBLOCK), :], vp_ref[pl.ds(start, SB_KBLOCK), :]
        b = pl.program_id(0)
        copies = [pltpu.make_async_copy(src.at[layer, b, :, pl.ds(start, SB_KBLOCK)], dst, sem.at[n])
                  for n, (src, dst) in enumerate(((kc_hbm, kbuf_ref), (vc_hbm, vbuf_ref)))]
        for cp in copies:
            cp.start()
        for cp in copies:
            cp.wait()
        return kbuf_ref[...].T.astype(BF16), vbuf_ref[...].T.astype(BF16)
    outs, prefs = _sb_step(q_tiles, kws, vws, w_ref[...], None, lims, cmr_ref[...])

    def store(u, heads_out):
        o_ref[0, u * tq:(u + 1) * tq, :] = jnp.concatenate(heads_out, axis=1).astype(BF16)

    rows = SB_HEADS * tq
    nblks = [r // SB_KBLOCK if isinstance(r, int) else lax.shift_right_logical(r, 7) for r in rests]
    alive = []
    for u in range(nsub):
        store(u, outs[u])
        alive.append(jnp.logical_and(nblks[u] > 0, jnp.max(prefs[u * rows:(u + 1) * rows]) > SB_DEAD_LOG))

    @pl.when(functools.reduce(jnp.logical_or, alive))
    def _():
        for u in range(nsub):
            def cond(carry, nblk=nblks[u]):
                return jnp.logical_and(carry[0] < nblk, carry[1])

            def body(carry, rest=rests[u], u=u):
                j, pref = carry[0], carry[2]
                start = pl.multiple_of(rest - (j + 1) * SB_KBLOCK, SB_KBLOCK)
                kb, vb = past_block(start)
                o_new, p_new = _sb_step([q_tiles[u]], [kb], [vb], w_ref[...], pref, None, None)
                pref = pref + p_new
                return (j + 1, jnp.max(pref) > SB_DEAD_LOG, pref,
                        *[o + n for o, n in zip(carry[3:], o_new[0])])

            final = lax.while_loop(cond, body, (jnp.int32(0), alive[u], prefs[u * rows:(u + 1) * rows],
                                                *outs[u]))
            store(u, final[3:])


def _sb(sq, k_new, v_new, k_past, v_past, layer, past_is_self):
    B, T, _ = sq.shape
    P = k_past.shape[2] if past_is_self else k_past.shape[3]
    tq = min(SB_KBLOCK, T)
    nsub = SB_TILES_PER_STEP if past_is_self and T % (SB_TILES_PER_STEP * tq) == 0 else 1
    n = T // (nsub * tq)
    wp = min(SB_WINDOW_PAST, P - tq if past_is_self else P)
    tw = -(-(wp + tq) // LANES) * LANES
    assert T % tq == 0 and tq & (tq - 1) == 0 and P % SB_KBLOCK == 0 and wp % SB_KBLOCK == 0
    assert T == tq or tq == SB_KBLOCK
    assert not past_is_self or tw == wp + tq
    row = pl.BlockSpec((1, nsub * tq, SB_W), lambda b, i: (b, i, 0))
    const = lambda a: pl.BlockSpec(a.shape, lambda b, i: (0, 0), pipeline_mode=pl.Buffered(1))
    w = _suffix_weights()
    cmr = jnp.asarray(np.arange(tw)[None, :] - (np.arange(SB_HEADS * tq) % tq)[:, None], jnp.int32)
    if past_is_self:
        full = pl.BlockSpec((1, 1, P, SB_W), lambda b, i: (0, b, 0, 0), pipeline_mode=pl.Buffered(1))
        args = [sq, k_past, v_past, w, cmr]
        specs = [row, full, full, const(w), const(cmr)]
        scratch = []
    else:
        assert P % wp == 0
        win = pl.BlockSpec((1, 1, SB_W, wp), lambda b, i: (layer, b, 0, P // wp - 1))
        hbm = pl.BlockSpec(memory_space=pl.ANY)
        args = [sq, k_new, v_new, k_past, v_past, k_past, v_past, w, cmr]
        specs = [row, row, row, win, win, hbm, hbm, const(w), const(cmr)]
        scratch = [pltpu.VMEM((SB_W, SB_KBLOCK), F32), pltpu.VMEM((SB_W, SB_KBLOCK), F32),
                   pltpu.SemaphoreType.DMA((2,))]
    return pl.pallas_call(
        functools.partial(_sb_kernel, tq, nsub, tw, wp, P, layer, past_is_self),
        grid=(B, n),
        in_specs=specs,
        out_specs=row,
        out_shape=jax.ShapeDtypeStruct((B, T, SB_W), BF16),
        scratch_shapes=scratch,
        compiler_params=pltpu.CompilerParams(dimension_semantics=("arbitrary", "arbitrary"),
                                             vmem_limit_bytes=VMEM_LIMIT_BYTES),
        name="sb",
    )(*args)


FF_CHUNK = 1024
POST_PART_ROWS = 256


def _post_kernel(pre_ln, x_ref, go_ref, so_ref, co_ref, lng_ref, lnb_ref, wo_ref, g1_ref, b1_ref,
                 wu_ref, wd_ref, g2_ref, b2_ref, y_ref):
    tm = x_ref.shape[0]
    n_ff = D_FF // FF_CHUNK
    if tm % POST_PART_ROWS == 0 and POST_PART_ROWS % (n_ff * SUBLANES) == 0:
        halves = [(r, POST_PART_ROWS) for r in range(0, tm, POST_PART_ROWS)]
    else:
        halves = [(0, tm)]

    def mix_rows(r0, n):
        rs = slice(r0, r0 + n)
        return (jnp.dot(go_ref[rs, :], wo_ref[0:GLA_W, :], preferred_element_type=F32)
                + jnp.dot(so_ref[rs, :], wo_ref[GLA_W:GLA_W + SB_W, :], preferred_element_type=F32)
                + jnp.dot(co_ref[rs, :], wo_ref[GLA_W + SB_W:MIX_W, :], preferred_element_type=F32))

    def ln1_rows(mix, r0, n, off):
        x = x_ref[r0 + off:r0 + off + n, :]
        if pre_ln:
            x = _layer_norm(x, lng_ref[...], lnb_ref[...])
        return _layer_norm(DEEPNORM_ALPHA * x + mix[off:off + n], g1_ref[...], b1_ref[...])

    def ffn_chunk(x1b, c):
        c0 = c * FF_CHUNK
        h = jnp.dot(x1b, wu_ref[:, c0:c0 + FF_CHUNK], preferred_element_type=F32)
        h = jnp.square(jnp.maximum(h, 0.0)).astype(BF16)
        return jnp.dot(h, wd_ref[c0:c0 + FF_CHUNK, :], preferred_element_type=F32)

    def ln2_rows(x1, ff, r0, n, off):
        y_ref[r0 + off:r0 + off + n, :] = _layer_norm(DEEPNORM_ALPHA * x1[off:off + n] + ff[off:off + n],
                                                     g2_ref[...], b2_ref[...])

    r0, n = halves[0]
    x1 = ln1_rows(mix_rows(r0, n), r0, n, 0)
    prev = None
    for hi in range(len(halves)):
        r0, n = halves[hi]
        nxt = halves[hi + 1] if hi + 1 < len(halves) else None
        nxt_mix = mix_rows(*nxt) if nxt else None
        x1b = x1.astype(BF16)
        g = n // n_ff
        ff, nxt_x1 = None, []
        for c in range(n_ff):
            part = ffn_chunk(x1b, c)
            ff = part if ff is None else ff + part
            if prev is not None:
                ln2_rows(prev[0], prev[1], prev[2], g, c * g)
            if nxt:
                nxt_x1.append(ln1_rows(nxt_mix, nxt[0], g, c * g))
        prev = (x1, ff, r0, n)
        if nxt:
            x1 = jnp.concatenate(nxt_x1, axis=0)
    ln2_rows(prev[0], prev[1], prev[2], prev[3], 0)


def _post(x, go, so, co, pre_ln, ln_g, ln_b, wo, g1, b1, wu, wd, g2, b2):
    N, D = x.shape
    tm = min(DENSE_TILE_ROWS, N)
    row = lambda width: pl.BlockSpec((tm, width), lambda i: (i, 0))
    const = lambda a: pl.BlockSpec(a.shape, lambda i: (0, 0), pipeline_mode=pl.Buffered(1))
    return pl.pallas_call(
        functools.partial(_post_kernel, pre_ln),
        grid=(N // tm,),
        in_specs=[row(D), row(GLA_W), row(SB_W), row(CONV_CH), const(ln_g), const(ln_b), const(wo),
                  const(g1), const(b1), const(wu), const(wd), const(g2), const(b2)],
        out_specs=row(D),
        out_shape=jax.ShapeDtypeStruct((N, D), F32),
        compiler_params=pltpu.CompilerParams(dimension_semantics=("arbitrary",),
                                             vmem_limit_bytes=VMEM_LIMIT_BYTES),
        name="post",
    )(x, go, so, co, ln_g, ln_b, wo, g1, b1, wu, wd, g2, b2)


def _prep_layer(l, w_in, w_gate_up, b_gate, gla_norm_g, conv_w, w_out, ln1_g, ln1_b, w_up, w_down,
                ln2_g, ln2_b):
    wl = w_in[l]
    w = jnp.concatenate([wl[:, :_GLOW_SRC], wl[:, _GLOW_SRC + GLA_GATE_RANK:],
                         wl[:, _GLOW_SRC:_GLOW_SRC + GLA_GATE_RANK],
                         jnp.zeros((D_MODEL, LANES - GLA_GATE_RANK), F32)], axis=1).astype(BF16)
    wg = jnp.concatenate([w_gate_up[l], jnp.zeros((LANES - GLA_GATE_RANK, GLA_KW), F32)], axis=0).astype(BF16)
    r2 = lambda a: a.reshape(1, -1)
    return dict(w=w, wg=wg, bg=r2(b_gate[l]), gn=r2(gla_norm_g[l]), cw=conv_w[l],
                wo=w_out[l].astype(BF16), g1=r2(ln1_g[l]), b1=r2(ln1_b[l]),
                wu=w_up[l].astype(BF16), wd=w_down[l].astype(BF16), g2=r2(ln2_g[l]), b2=r2(ln2_b[l]))


def _trunk(x, c_k, c_v, s_gla, s_conv, ln_g, ln_b, layers):
    B, T, D = x.shape
    ks, vs, gs, cs = [], [], [], []
    kv_all = None
    for l, p in enumerate(layers):
        pre_ln = l == 0
        if c_k is None:
            s0 = jnp.zeros((B, GLA_HEADS, GLA_DK, GLA_DV), F32)
            cprev = jnp.zeros((B, CONV_W - 1, CONV_CH), F32)
            kv_args = dict(kv_layer=l, kv_prev=kv_all)
        else:
            s0, cprev = s_gla[l], s_conv[l]
            kv_args = {}
        gq, gk, gv, gr, lg, sq, skf, svf, skb, svb, co, cst = _in_proj(
            x, pre_ln, ln_g, ln_b, p["w"], p["wg"], p["bg"], p["cw"], cprev, **kv_args)
        if c_k is None:
            kv_all = (skf, svf)
        go, s_fin = _gla(gq, gk, gv, lg, gr, p["gn"], s0)
        if c_k is None:
            so = _sb(sq, skb, svb, skb[None], svb[None], 0, True)
        else:
            key_minor = lambda c: jnp.transpose(c, (0, 1, 3, 4, 2)).reshape(DEPTH, B, SB_W, -1)
            so = _sb(sq, skb, svb, key_minor(c_k), key_minor(c_v), l, False)
        x = _post(x.reshape(B * T, D), go.reshape(B * T, GLA_W), so.reshape(B * T, SB_W),
                  co.reshape(B * T, CONV_CH), pre_ln, ln_g, ln_b, p["wo"], p["g1"], p["b1"],
                  p["wu"], p["wd"], p["g2"], p["b2"]).reshape(B, T, D)
        ks.append(skf)
        vs.append(svf)
        gs.append(s_fin)
        cs.append(cst)
    if kv_all is not None:
        k_out, v_out = (jnp.transpose(a.reshape(DEPTH, B, SB_HEADS, SB_DH, T), (0, 1, 4, 2, 3)) for a in kv_all)
    else:
        k_out, v_out = (jnp.stack(a).reshape(DEPTH, B, T, SB_HEADS, SB_DH) for a in (ks, vs))
    return x, k_out, v_out, jnp.stack(gs), jnp.stack(cs)


def kernel(x_prompt, x_sample, cache_sb_k, cache_sb_v, state_gla, state_conv, ln_in_g, ln_in_b, w_in,
           w_gate_up, b_gate, gla_norm_g, conv_w, w_out, ln1_g, ln1_b, w_up, w_down, ln2_g, ln2_b):
    layers = [_prep_layer(l, w_in, w_gate_up, b_gate, gla_norm_g, conv_w, w_out, ln1_g, ln1_b, w_up,
                          w_down, ln2_g, ln2_b) for l in range(DEPTH)]
    ln_g, ln_b = ln_in_g.reshape(1, -1), ln_in_b.reshape(1, -1)
    y_p, k_p, v_p, gla_p, conv_p = _trunk(x_prompt, None, None, None, None, ln_g, ln_b, layers)
    y_s, k_s, v_s, gla_s, conv_s = _trunk(x_sample, cache_sb_k, cache_sb_v, state_gla, state_conv,
                                          ln_g, ln_b, layers)
    return (y_p, y_s, k_p, v_p, gla_p, conv_p, k_s, v_s, gla_s, conv_s)
```

```python
import functools

import numpy as np

import jax
import jax.numpy as jnp
from jax import lax
from jax.experimental import pallas as pl
from jax.experimental.pallas import tpu as pltpu

F32 = jnp.float32
BF16 = jnp.bfloat16

D_MODEL = 1024
DEPTH = 2
CHUNK = 64
GLA_HEADS = 4
GLA_DK = 64
GLA_DV = 128
GLA_GATE_RANK = 16
GLA_TAU = 16.0
GLA_W = GLA_HEADS * GLA_DV
GLA_KW = GLA_HEADS * GLA_DK
SB_HEADS = 4
SB_DH = 64
SB_W = SB_HEADS * SB_DH
CONV_CH = 256
CONV_W = 3
MIX_W = GLA_W + SB_W + CONV_CH
D_FF = 4 * D_MODEL
LN_EPS = 1e-5
DEEPNORM_ALPHA = (2 * DEPTH) ** 0.25
LOG2_E = 1.4426950408889634

LANES = 128
SUBLANES = 8
SB_KBLOCK = 128
SB_DEAD_LOG = -104.0
VMEM_LIMIT_BYTES = 56 * 1024 * 1024
DENSE_TILE_ROWS = 1024
SB_TILES_PER_STEP = 8

_C_GQ = 0
_C_GK = _C_GQ + GLA_KW
_C_GV = _C_GK + GLA_KW
_C_GR = _C_GV + GLA_W
_C_SQ = _C_GR + GLA_W
_C_SK = _C_SQ + SB_W
_C_SV = _C_SK + SB_W
_C_CB = _C_SV + SB_W
_C_CC = _C_CB + CONV_CH
_C_CH = _C_CC + CONV_CH
_C_GLOW = _C_CH + CONV_CH
N_IN_PAD = _C_GLOW + LANES
_GLOW_SRC = 2 * GLA_KW + 2 * GLA_W


def _layer_norm(x, g, b):
    mu = jnp.mean(x, axis=-1, keepdims=True)
    xc = x - mu
    var = jnp.mean(xc * xc, axis=-1, keepdims=True)
    return xc * lax.rsqrt(var + LN_EPS) * g + b


def _softplus_neg_abs(z):
    return jnp.log1p(jnp.exp(-jnp.abs(z)))


def _split_bf16(x, n):
    parts = []
    r = x
    for _ in range(n - 1):
        p = r.astype(BF16)
        parts.append(p)
        r = r - p.astype(F32)
    parts.append(r.astype(BF16))
    return parts


GLA_LEVELS = (32, 16, 8, 4, 2, 1)
ATT_W = GLA_HEADS * CHUNK
B_PAD = SUBLANES
(_K_NEG_UP, _K_NEG_LO, _K_PAIR) = (0, len(GLA_LEVELS), 2 * len(GLA_LEVELS))
_K_EYE = 3 * len(GLA_LEVELS)
_K_POS4 = _K_EYE + 1
_K_ODD = _K_POS4 + 4
_K_COUNT = _K_ODD + 1


def _gla_constants():
    assert ATT_W == GLA_KW
    t = np.arange(CHUNK)[:, None]
    s = (np.arange(ATT_W) % CHUNK)[None, :]
    ninf = np.float32(-np.inf)
    out = np.zeros((_K_COUNT, CHUNK, ATT_W), np.float32)
    for li, h in enumerate(GLA_LEVELS):
        upper = np.broadcast_to((t & h) != 0, (CHUNK, ATT_W))
        out[_K_NEG_UP + li] = np.where(upper, 0.0, ninf)
        out[_K_NEG_LO + li] = np.where(upper, ninf, 0.0)
        out[_K_PAIR + li] = (((t ^ s) // (2 * h)) == 0) & upper & ((s & h) == 0)
    out[_K_EYE] = t == s
    for p in range(4):
        out[_K_POS4 + p] = np.broadcast_to((t & 3) == p, (CHUNK, ATT_W))
    out[_K_ODD] = np.broadcast_to((t & 1) == 1, (CHUNK, ATT_W))
    tril = (np.arange(CHUNK)[None, :] <= np.arange(CHUNK)[:, None]).astype(np.float32)
    hk = (np.arange(ATT_W)[:, None] // CHUNK == np.arange(GLA_KW)[None, :] // GLA_DK).astype(np.float32)
    hv = (np.arange(ATT_W)[:, None] // CHUNK == np.arange(GLA_W)[None, :] // GLA_DV).astype(np.float32)
    return (jnp.asarray(out), jnp.asarray(tril, BF16), jnp.asarray(hk, BF16), jnp.asarray(hv, BF16))


def _gla_stages(q2, k2, v2, lg2, gr2, st, emit, cst_ref, tril_ref, hk_ref, hv_ref, gn_ref, b_ref):
    C = CHUNK
    T = q2.shape[0]
    nc = T // C
    nt = (((1,), (1,)), ((), ()))
    tn = (((0,), (0,)), ((), ()))
    bnn = (((2,), (1,)), ((0,), (0,)))
    bnt = (((2,), (2,)), ((0,), (0,)))
    d = {}

    def chunks(a):
        return a.reshape(nc, C, a.shape[-1])

    q, k, v = chunks(q2), chunks(k2), chunks(v2)

    def decay():
        tril = jnp.broadcast_to(tril_ref[...], (nc, C, C))
        b = sum(lax.dot_general(tril, chunks(p), bnn, preferred_element_type=F32)
                for p in _split_bf16(lg2, 3))
        b_ref[pl.ds(B_PAD, T), :] = b.reshape(T, GLA_KW)
        d["b"] = b

    def carried_operands():
        b = d["b"]
        b_last = b[:, C - 1:C, :]
        d["qe"] = (q * jnp.exp2(b)).astype(BF16)
        d["kd"] = (k * jnp.exp2(b_last - b)).astype(BF16)
        d["e_last"] = jnp.exp2(b_last)

    def scaled_pair(li, h):
        b = d["b"]
        if h >= SUBLANES:
            zeros = jnp.zeros((nc, h, GLA_KW), F32)
            qs, ks_ = [], []
            for m in range(C // (2 * h)):
                lo = slice(m * 2 * h, m * 2 * h + h)
                up = slice(m * 2 * h + h, (m + 1) * 2 * h)
                r = b[:, m * 2 * h + h - 1:m * 2 * h + h, :]
                qs += [zeros, q[:, up] * jnp.exp2(b[:, up] - r)]
                ks_ += [k[:, lo] * jnp.exp2(r - b[:, lo]), zeros]
            return jnp.concatenate(qs, axis=1).astype(BF16), jnp.concatenate(ks_, axis=1).astype(BF16)
        if h == 4:
            r = jnp.concatenate([jnp.broadcast_to(b[:, m * 8 + 3:m * 8 + 4, :], (nc, 8, GLA_KW))
                                 for m in range(C // 8)], axis=1)
        else:
            bm1 = chunks(b_ref[pl.ds(B_PAD - 1, T), :])
            if h == 1:
                r = b + cst_ref[_K_ODD] * (bm1 - b)
            else:
                bm2 = chunks(b_ref[pl.ds(B_PAD - 2, T), :])
                bp1 = chunks(b_ref[pl.ds(B_PAD + 1, T), :])
                r = ((cst_ref[_K_POS4 + 0] * bp1 + cst_ref[_K_POS4 + 1] * b)
                     + (cst_ref[_K_POS4 + 2] * bm1 + cst_ref[_K_POS4 + 3] * bm2))
        return ((q * jnp.exp2((b - r) + cst_ref[_K_NEG_UP + li])).astype(BF16),
                (k * jnp.exp2((r - b) + cst_ref[_K_NEG_LO + li])).astype(BF16))

    def diagonal():
        d["att"] = (chunks(jnp.dot((q2 * k2).astype(BF16), hk_ref[...], preferred_element_type=F32))
                    * cst_ref[_K_EYE])

    def level(li, h):
        qt, kt = scaled_pair(li, h)
        kbd = jnp.concatenate([kt] * GLA_HEADS, axis=1) * hk_ref[...]
        a = lax.dot_general(qt, kbd, bnt, preferred_element_type=F32)
        d["att"] = d["att"] + (a if 2 * h == C else a * cst_ref[_K_PAIR + li])

    def within_chunk():
        vbd = jnp.concatenate([v] * GLA_HEADS, axis=1) * hv_ref[...]
        d["o"] = lax.dot_general(d.pop("att").astype(BF16), vbd, bnn, preferred_element_type=F32)

    def recurrence():
        qe, kd, e_last = d.pop("qe"), d.pop("kd"), d.pop("e_last")
        o_inter = []
        for c in range(nc):
            per_head = []
            for h in range(GLA_HEADS):
                ks = slice(h * GLA_DK, (h + 1) * GLA_DK)
                vs = slice(h * GLA_DV, (h + 1) * GLA_DV)
                per_head.append(lax.dot_general(qe[c][:, ks], st[h].astype(BF16), nt,
                                                preferred_element_type=F32))
                st[h] = st[h] * e_last[c][:, ks] + lax.dot_general(v[c][:, vs], kd[c][:, ks], tn,
                                                                   preferred_element_type=F32)
            o_inter.append(jnp.concatenate(per_head, axis=1))
        d["o"] = d["o"].reshape(T, GLA_W) + jnp.concatenate(o_inter, axis=0)

    def gate():
        o = d.pop("o")
        normed = []
        for h in range(GLA_HEADS):
            oh = o[:, h * GLA_DV:(h + 1) * GLA_DV]
            normed.append(oh * lax.rsqrt(jnp.mean(oh * oh, axis=-1, keepdims=True) + LN_EPS))
        emit(jnp.concatenate(normed, axis=1) * gn_ref[...] * (gr2 * jax.nn.sigmoid(gr2)))

    return ([decay, carried_operands, diagonal]
            + [functools.partial(level, li, h) for li, h in enumerate(GLA_LEVELS)]
            + [within_chunk, recurrence, gate])


IN_PROJ_PART_ROWS = 256
_N_IN_PROJ_INPUTS = 14


def _in_proj_kernel(pre_ln, tm, kv_slabs, n_aliased, *refs):
    (x_ref, lng_ref, lnb_ref, w_ref, wg_ref, bg_ref, cw_ref, cprev_ref, gn_ref, s0_ref,
     cst_ref, tril_ref, hk_ref, hv_ref) = refs[:_N_IN_PROJ_INPUTS]
    (go_ref, sfin_ref, sq_ref, skf_ref, svf_ref, skb_ref, svb_ref, co_ref, cst_out_ref,
     uext_ref, st_ref, b_ref) = refs[_N_IN_PROJ_INPUTS + n_aliased:]
    i = pl.program_id(1)

    @pl.when(i == 0)
    def _():
        uext_ref[SUBLANES - 2:SUBLANES, :] = cprev_ref[0]
        for h in range(GLA_HEADS):
            st_ref[h] = s0_ref[0, h].T
        b_ref[...] = jnp.zeros(b_ref.shape, F32)

    part = IN_PROJ_PART_ROWS if tm % IN_PROJ_PART_ROWS == 0 else tm

    def normed_rows(r0):
        x = x_ref[0, r0:r0 + part, :]
        if pre_ln:
            x = _layer_norm(x, lng_ref[...], lnb_ref[...])
        return x.astype(BF16)

    cw = cw_ref[...]
    st = [st_ref[h] for h in range(GLA_HEADS)]
    pending = []

    def tick():
        if pending:
            pending.pop(0)()

    def store_go(rs, out):
        go_ref[0, rs, :] = out.astype(BF16)

    xb = normed_rows(0)
    for pi, r0 in enumerate(range(0, tm, part)):
        rs = slice(r0, r0 + part)

        def proj(a, b, xb=xb):
            return jnp.dot(xb, w_ref[:, a:b], preferred_element_type=F32)

        glow = proj(_C_GLOW, N_IN_PAD)
        u = jnp.dot(glow.astype(BF16), wg_ref[...], preferred_element_type=F32) + bg_ref[...]
        lg = (jnp.minimum(u, 0.0) - _softplus_neg_abs(u)) * (LOG2_E / GLA_TAU)
        tick()

        cb = proj(_C_CB, _C_CC)
        tick()
        uc = proj(_C_CC, _C_CH) * proj(_C_CH, _C_GLOW)
        uext_ref[SUBLANES + r0:SUBLANES + r0 + part, :] = uc
        um1 = uext_ref[SUBLANES - 1 + r0:SUBLANES - 1 + r0 + part, :]
        um2 = uext_ref[SUBLANES - 2 + r0:SUBLANES - 2 + r0 + part, :]
        cy = (cw[2:3, :] * uc + cw[0:1, :] * um2) + cw[1:2, :] * um1
        co_ref[0, rs, :] = (cb * cy).astype(BF16)
        tick()

        gq = proj(_C_GQ, _C_GK) * (GLA_DK ** -0.5)
        tick()
        gk = proj(_C_GK, _C_GV)
        tick()
        gv = proj(_C_GV, _C_GR).astype(BF16)
        tick()
        xb_next = normed_rows(r0 + part) if r0 + part < tm else None
        tick()
        gr = proj(_C_GR, _C_SQ)
        tick()
        sq_ref[0, rs, :] = (proj(_C_SQ, _C_SK) * (SB_DH ** -0.5)).astype(BF16)
        tick()
        for c0, c1, f_ref, h_ref in ((_C_SK, _C_SV, skf_ref, skb_ref), (_C_SV, _C_CB, svf_ref, svb_ref)):
            kv = proj(c0, c1)
            h_ref[0, rs, :] = kv.astype(BF16)
            if kv_slabs == 0:
                f_ref[0, rs, :] = kv
            else:
                kv_t = kv.T
                for slab in range(kv_slabs):
                    f_ref[slab, 0, :, rs] = kv_t
            tick()
        while pending:
            tick()
        pending.extend(_gla_stages(gq, gk, gv, lg, gr, st, functools.partial(store_go, rs), cst_ref, tril_ref,
                                   hk_ref, hv_ref, gn_ref, b_ref.at[pi]))
        xb = xb_next
    while pending:
        tick()
    tail = uc[part - 2:part, :]
    uext_ref[SUBLANES - 2:SUBLANES, :] = tail
    for h in range(GLA_HEADS):
        st_ref[h] = st[h]

    @pl.when(i == pl.num_programs(1) - 1)
    def _():
        cst_out_ref[0] = tail
        for h in range(GLA_HEADS):
            sfin_ref[0, h] = st[h].T


def _in_proj(x, pre_ln, ln_g, ln_b, w, wg, bg, cw, cprev, gn, s0, kv_layer=None, kv_prev=None):
    B, T, D = x.shape
    tm = min(DENSE_TILE_ROWS, T)
    n = T // tm
    part = IN_PROJ_PART_ROWS if tm % IN_PROJ_PART_ROWS == 0 else tm
    assert part % CHUNK == 0
    row = lambda width: pl.BlockSpec((1, tm, width), lambda b, i: (b, i, 0))
    const = lambda a: pl.BlockSpec(a.shape, lambda b, i: (0,) * a.ndim, pipeline_mode=pl.Buffered(1))
    state = pl.BlockSpec((1, GLA_HEADS, GLA_DK, GLA_DV), lambda b, i: (b, 0, 0, 0))
    conv_state = pl.BlockSpec((1, CONV_W - 1, CONV_CH), lambda b, i: (b, 0, 0))
    stacked = kv_layer is not None
    assert (kv_prev is None) == (not stacked or kv_layer == 0)
    aliased = [] if kv_prev is None else list(kv_prev)
    kv_slabs = 0 if not stacked else (DEPTH if kv_layer == 0 else 1)
    kv_shape = (DEPTH, B, SB_W, T) if stacked else (B, T, SB_W)
    kv_spec = (pl.BlockSpec((kv_slabs, 1, SB_W, tm), lambda b, i: (kv_layer, b, 0, i)) if stacked
               else row(SB_W))
    consts = _gla_constants()
    inputs = [x, ln_g, ln_b, w, wg, bg, cw, cprev, gn, s0, *consts]
    assert len(inputs) == _N_IN_PROJ_INPUTS
    in_specs = ([row(D), const(ln_g), const(ln_b), const(w), const(wg), const(bg), const(cw), conv_state,
                 const(gn), state] + [const(c) for c in consts]
                + [pl.BlockSpec(memory_space=pl.ANY)] * len(aliased))
    outs = [
        (jax.ShapeDtypeStruct((B, T, GLA_W), BF16), row(GLA_W)),
        (jax.ShapeDtypeStruct((B, GLA_HEADS, GLA_DK, GLA_DV), F32), state),
        (jax.ShapeDtypeStruct((B, T, SB_W), BF16), row(SB_W)),
        (jax.ShapeDtypeStruct(kv_shape, F32), kv_spec),
        (jax.ShapeDtypeStruct(kv_shape, F32), kv_spec),
        (jax.ShapeDtypeStruct((B, T, SB_W), BF16), row(SB_W)),
        (jax.ShapeDtypeStruct((B, T, SB_W), BF16), row(SB_W)),
        (jax.ShapeDtypeStruct((B, T, CONV_CH), BF16), row(CONV_CH)),
        (jax.ShapeDtypeStruct((B, CONV_W - 1, CONV_CH), F32), conv_state),
    ]
    return pl.pallas_call(
        functools.partial(_in_proj_kernel, pre_ln, tm, kv_slabs, len(aliased)),
        grid=(B, n),
        in_specs=in_specs,
        out_specs=[o[1] for o in outs],
        out_shape=[o[0] for o in outs],
        input_output_aliases={_N_IN_PROJ_INPUTS + j: 3 + j for j in range(len(aliased))},
        scratch_shapes=[pltpu.VMEM((SUBLANES + tm, CONV_CH), F32),
                        pltpu.VMEM((GLA_HEADS, GLA_DV, GLA_DK), F32),
                        pltpu.VMEM((tm // part, part + 2 * B_PAD, GLA_KW), F32)],
        compiler_params=pltpu.CompilerParams(dimension_semantics=("arbitrary", "arbitrary"),
                                             vmem_limit_bytes=VMEM_LIMIT_BYTES),
        name="in_proj",
    )(*inputs, *aliased)


SB_WINDOW_PAST = 256


def _suffix_weights():
    j = (np.arange(2 * LANES) % LANES)[:, None]
    s = np.arange(2 * LANES)[None, :]
    return jnp.asarray((j > s) | (s >= LANES), BF16)


SB_MASKED_LOGIT = -1e30


def _sb_step(q_tiles, kws, vws, w, prefix, lims, col_minus_row):
    tq, tw = q_tiles[0].shape[0], kws[0].shape[0]
    n_tiles = len(q_tiles)
    nt = (((1,), (1,)), ((), ()))
    heads = [slice(h * SB_DH, (h + 1) * SB_DH) for h in range(SB_HEADS)]
    rows = SB_HEADS * tq
    st = [dict() for _ in range(n_tiles)]

    def logits(u):
        z = jnp.concatenate([lax.dot_general(q_tiles[u][:, hs], kws[u][:, hs], nt,
                                             preferred_element_type=F32) for hs in heads], axis=0)
        if lims is not None:
            z = jnp.where(col_minus_row < lims[u], z, SB_MASKED_LOGIT)
        st[u]["z"] = z

    def logs(u):
        z = st[u].pop("z")
        log_beta = jnp.minimum(z, 0.0) - jnp.log(1.0 + jnp.exp(-jnp.abs(z)))
        st[u]["log_beta"] = log_beta
        st[u]["keep"] = _split_bf16(log_beta - z, 2)

    def suffix(u):
        hi, lo = st[u].pop("keep")
        before = None if prefix is None else prefix[u * rows:(u + 1) * rows]
        after = [None] * (tw // LANES)
        total = None
        for blk in reversed(range(tw // LANES)):
            cs = slice(blk * LANES, (blk + 1) * LANES)
            res = jnp.dot(jnp.concatenate([hi[:, cs], lo[:, cs]], axis=1), w, preferred_element_type=F32)
            later = total if before is None else (before if total is None else before + total)
            after[blk] = res[:, :LANES] if later is None else res[:, :LANES] + later
            total = res[:, LANES:] if total is None else total + res[:, LANES:]
        st[u]["after"] = after
        st[u]["total"] = total

    def weights(u):
        after = st[u].pop("after")
        after = after[0] if len(after) == 1 else jnp.concatenate(after, axis=1)
        st[u]["a"] = jnp.exp(st[u].pop("log_beta") + after).astype(BF16)

    def values(u):
        a = st[u].pop("a")
        st[u]["out"] = [jnp.dot(a[h * tq:(h + 1) * tq], vws[u][:, hs], preferred_element_type=F32)
                        for h, hs in enumerate(heads)]

    stages = (logits, logs, suffix, weights, values)
    for step in range(n_tiles + len(stages) - 1):
        for depth in reversed(range(len(stages))):
            if 0 <= step - depth < n_tiles:
                stages[depth](step - depth)
    totals = [s["total"] for s in st]
    return [s["out"] for s in st], totals[0] if n_tiles == 1 else jnp.concatenate(totals, axis=0)


def _sb_kernel(tq, nsub, tw, wp, n_past, layer, past_is_self, *refs):
    if past_is_self:
        q_ref, kp_ref, vp_ref, w_ref, cmr_ref, o_ref = refs
        kp_ref = kp_ref.at[0, 0]
        vp_ref = vp_ref.at[0, 0]
    else:
        (q_ref, kd_ref, vd_ref, kwin_ref, vwin_ref, kc_hbm, vc_hbm, w_ref, cmr_ref, o_ref,
         kbuf_ref, vbuf_ref, sem) = refs
    i = pl.program_id(1)
    q_tiles = [q_ref[0, u * tq:(u + 1) * tq, :] for u in range(nsub)]

    if past_is_self:
        q0s = [(i * nsub + u) * tq for u in range(nsub)]
        rests = [pl.multiple_of(jnp.maximum(q0 - wp, 0), SB_KBLOCK) for q0 in q0s]
        kws = [kp_ref[pl.ds(r, tw), :] for r in rests]
        vws = [vp_ref[pl.ds(r, tw), :] for r in rests]
        lims = [q0 - r for q0, r in zip(q0s, rests)]
    else:
        rests = [n_past - wp]
        pad = jnp.zeros((tw - wp - tq, SB_W), BF16)
        kws = [jnp.concatenate([kwin_ref[0, 0].T.astype(BF16), kd_ref[0], pad], axis=0)]
        vws = [jnp.concatenate([vwin_ref[0, 0].T.astype(BF16), vd_ref[0], pad], axis=0)]
        lims = [wp]

    def past_block(start):
        if past_is_self:
            return kp_ref[pl.ds(start, SB_KBLOCK), :], vp_ref[pl.ds(start, SB_KBLOCK), :]
        b = pl.program_id(0)
        copies = [pltpu.make_async_copy(src.at[layer, b, :, pl.ds(start, SB_KBLOCK)], dst, sem.at[n])
                  for n, (src, dst) in enumerate(((kc_hbm, kbuf_ref), (vc_hbm, vbuf_ref)))]
        for cp in copies:
            cp.start()
        for cp in copies:
            cp.wait()
        return kbuf_ref[...].T.astype(BF16), vbuf_ref[...].T.astype(BF16)
    outs, prefs = _sb_step(q_tiles, kws, vws, w_ref[...], None, lims, cmr_ref[...])

    def store(u, heads_out):
        o_ref[0, u * tq:(u + 1) * tq, :] = jnp.concatenate(heads_out, axis=1).astype(BF16)

    rows = SB_HEADS * tq
    nblks = [r // SB_KBLOCK if isinstance(r, int) else lax.shift_right_logical(r, 7) for r in rests]
    alive = []
    for u in range(nsub):
        store(u, outs[u])
        alive.append(jnp.logical_and(nblks[u] > 0, jnp.max(prefs[u * rows:(u + 1) * rows]) > SB_DEAD_LOG))

    @pl.when(functools.reduce(jnp.logical_or, alive))
    def _():
        for u in range(nsub):
            def cond(carry, nblk=nblks[u]):
                return jnp.logical_and(carry[0] < nblk, carry[1])

            def body(carry, rest=rests[u], u=u):
                j, pref = carry[0], carry[2]
                start = pl.multiple_of(rest - (j + 1) * SB_KBLOCK, SB_KBLOCK)
                kb, vb = past_block(start)
                o_new, p_new = _sb_step([q_tiles[u]], [kb], [vb], w_ref[...], pref, None, None)
                pref = pref + p_new
                return (j + 1, jnp.max(pref) > SB_DEAD_LOG, pref,
                        *[o + n for o, n in zip(carry[3:], o_new[0])])

            final = lax.while_loop(cond, body, (jnp.int32(0), alive[u], prefs[u * rows:(u + 1) * rows],
                                                *outs[u]))
            store(u, final[3:])


def _sb(sq, k_new, v_new, k_past, v_past, layer, past_is_self):
    B, T, _ = sq.shape
    P = k_past.shape[2] if past_is_self else k_past.shape[3]
    tq = min(SB_KBLOCK, T)
    nsub = SB_TILES_PER_STEP if past_is_self and T % (SB_TILES_PER_STEP * tq) == 0 else 1
    n = T // (nsub * tq)
    wp = min(SB_WINDOW_PAST, P - tq if past_is_self else P)
    tw = -(-(wp + tq) // LANES) * LANES
    assert T % tq == 0 and tq & (tq - 1) == 0 and P % SB_KBLOCK == 0 and wp % SB_KBLOCK == 0
    assert T == tq or tq == SB_KBLOCK
    assert not past_is_self or tw == wp + tq
    row = pl.BlockSpec((1, nsub * tq, SB_W), lambda b, i: (b, i, 0))
    const = lambda a: pl.BlockSpec(a.shape, lambda b, i: (0, 0), pipeline_mode=pl.Buffered(1))
    w = _suffix_weights()
    cmr = jnp.asarray(np.arange(tw)[None, :] - (np.arange(SB_HEADS * tq) % tq)[:, None], jnp.int32)
    if past_is_self:
        full = pl.BlockSpec((1, 1, P, SB_W), lambda b, i: (0, b, 0, 0), pipeline_mode=pl.Buffered(1))
        args = [sq, k_past, v_past, w, cmr]
        specs = [row, full, full, const(w), const(cmr)]
        scratch = []
    else:
        assert P % wp == 0
        win = pl.BlockSpec((1, 1, SB_W, wp), lambda b, i: (layer, b, 0, P // wp - 1))
        hbm = pl.BlockSpec(memory_space=pl.ANY)
        args = [sq, k_new, v_new, k_past, v_past, k_past, v_past, w, cmr]
        specs = [row, row, row, win, win, hbm, hbm, const(w), const(cmr)]
        scratch = [pltpu.VMEM((SB_W, SB_KBLOCK), F32), pltpu.VMEM((SB_W, SB_KBLOCK), F32),
                   pltpu.SemaphoreType.DMA((2,))]
    return pl.pallas_call(
        functools.partial(_sb_kernel, tq, nsub, tw, wp, P, layer, past_is_self),
        grid=(B, n),
        in_specs=specs,
        out_specs=row,
        out_shape=jax.ShapeDtypeStruct((B, T, SB_W), BF16),
        scratch_shapes=scratch,
        compiler_params=pltpu.CompilerParams(dimension_semantics=("arbitrary", "arbitrary"),
                                             vmem_limit_bytes=VMEM_LIMIT_BYTES),
        name="sb",
    )(*args)


FF_CHUNK = 1024
POST_PART_ROWS = 256


def _post_kernel(pre_ln, x_ref, go_ref, so_ref, co_ref, lng_ref, lnb_ref, wo_ref, g1_ref, b1_ref,
                 wu_ref, wd_ref, g2_ref, b2_ref, y_ref):
    tm = x_ref.shape[0]
    n_ff = D_FF // FF_CHUNK
    if tm % POST_PART_ROWS == 0 and POST_PART_ROWS % (n_ff * SUBLANES) == 0:
        halves = [(r, POST_PART_ROWS) for r in range(0, tm, POST_PART_ROWS)]
    else:
        halves = [(0, tm)]

    def mix_rows(r0, n):
        rs = slice(r0, r0 + n)
        return (jnp.dot(go_ref[rs, :], wo_ref[0:GLA_W, :], preferred_element_type=F32)
                + jnp.dot(so_ref[rs, :], wo_ref[GLA_W:GLA_W + SB_W, :], preferred_element_type=F32)
                + jnp.dot(co_ref[rs, :], wo_ref[GLA_W + SB_W:MIX_W, :], preferred_element_type=F32))

    def ln1_rows(mix, r0, n, off):
        x = x_ref[r0 + off:r0 + off + n, :]
        if pre_ln:
            x = _layer_norm(x, lng_ref[...], lnb_ref[...])
        return _layer_norm(DEEPNORM_ALPHA * x + mix[off:off + n], g1_ref[...], b1_ref[...])

    def ffn_chunk(x1b, c):
        c0 = c * FF_CHUNK
        h = jnp.dot(x1b, wu_ref[:, c0:c0 + FF_CHUNK], preferred_element_type=F32)
        h = jnp.square(jnp.maximum(h, 0.0)).astype(BF16)
        return jnp.dot(h, wd_ref[c0:c0 + FF_CHUNK, :], preferred_element_type=F32)

    def ln2_rows(x1, ff, r0, n, off):
        y_ref[r0 + off:r0 + off + n, :] = _layer_norm(DEEPNORM_ALPHA * x1[off:off + n] + ff[off:off + n],
                                                     g2_ref[...], b2_ref[...])

    r0, n = halves[0]
    x1 = ln1_rows(mix_rows(r0, n), r0, n, 0)
    prev = None
    for hi in range(len(halves)):
        r0, n = halves[hi]
        nxt = halves[hi + 1] if hi + 1 < len(halves) else None
        nxt_mix = mix_rows(*nxt) if nxt else None
        x1b = x1.astype(BF16)
        g = n // n_ff
        ff, nxt_x1 = None, []
        for c in range(n_ff):
            part = ffn_chunk(x1b, c)
            ff = part if ff is None else ff + part
            if prev is not None:
                ln2_rows(prev[0], prev[1], prev[2], g, c * g)
            if nxt:
                nxt_x1.append(ln1_rows(nxt_mix, nxt[0], g, c * g))
        prev = (x1, ff, r0, n)
        if nxt:
            x1 = jnp.concatenate(nxt_x1, axis=0)
    ln2_rows(prev[0], prev[1], prev[2], prev[3], 0)


def _post(x, go, so, co, pre_ln, ln_g, ln_b, wo, g1, b1, wu, wd, g2, b2):
    N, D = x.shape
    tm = min(DENSE_TILE_ROWS, N)
    row = lambda width: pl.BlockSpec((tm, width), lambda i: (i, 0))
    const = lambda a: pl.BlockSpec(a.shape, lambda i: (0, 0), pipeline_mode=pl.Buffered(1))
    return pl.pallas_call(
        functools.partial(_post_kernel, pre_ln),
        grid=(N // tm,),
        in_specs=[row(D), row(GLA_W), row(SB_W), row(CONV_CH), const(ln_g), const(ln_b), const(wo),
                  const(g1), const(b1), const(wu), const(wd), const(g2), const(b2)],
        out_specs=row(D),
        out_shape=jax.ShapeDtypeStruct((N, D), F32),
        compiler_params=pltpu.CompilerParams(dimension_semantics=("arbitrary",),
                                             vmem_limit_bytes=VMEM_LIMIT_BYTES),
        name="post",
    )(x, go, so, co, ln_g, ln_b, wo, g1, b1, wu, wd, g2, b2)


def _prep_layer(l, w_in, w_gate_up, b_gate, gla_norm_g, conv_w, w_out, ln1_g, ln1_b, w_up, w_down,
                ln2_g, ln2_b):
    wl = w_in[l]
    w = jnp.concatenate([wl[:, :_GLOW_SRC], wl[:, _GLOW_SRC + GLA_GATE_RANK:],
                         wl[:, _GLOW_SRC:_GLOW_SRC + GLA_GATE_RANK],
                         jnp.zeros((D_MODEL, LANES - GLA_GATE_RANK), F32)], axis=1).astype(BF16)
    wg = jnp.concatenate([w_gate_up[l], jnp.zeros((LANES - GLA_GATE_RANK, GLA_KW), F32)], axis=0).astype(BF16)
    r2 = lambda a: a.reshape(1, -1)
    return dict(w=w, wg=wg, bg=r2(b_gate[l]), gn=r2(gla_norm_g[l]), cw=conv_w[l],
                wo=w_out[l].astype(BF16), g1=r2(ln1_g[l]), b1=r2(ln1_b[l]),
                wu=w_up[l].astype(BF16), wd=w_down[l].astype(BF16), g2=r2(ln2_g[l]), b2=r2(ln2_b[l]))


def _trunk(x, c_k, c_v, s_gla, s_conv, ln_g, ln_b, layers):
    B, T, D = x.shape
    ks, vs, gs, cs = [], [], [], []
    kv_all = None
    for l, p in enumerate(layers):
        pre_ln = l == 0
        if c_k is None:
            s0 = jnp.zeros((B, GLA_HEADS, GLA_DK, GLA_DV), F32)
            cprev = jnp.zeros((B, CONV_W - 1, CONV_CH), F32)
            kv_args = dict(kv_layer=l, kv_prev=kv_all)
        else:
            s0, cprev = s_gla[l], s_conv[l]
            kv_args = {}
        go, s_fin, sq, skf, svf, skb, svb, co, cst = _in_proj(
            x, pre_ln, ln_g, ln_b, p["w"], p["wg"], p["bg"], p["cw"], cprev, p["gn"], s0, **kv_args)
        if c_k is None:
            kv_all = (skf, svf)
            so = _sb(sq, skb, svb, skb[None], svb[None], 0, True)
        else:
            key_minor = lambda c: jnp.transpose(c, (0, 1, 3, 4, 2)).reshape(DEPTH, B, SB_W, -1)
            so = _sb(sq, skb, svb, key_minor(c_k), key_minor(c_v), l, False)
        x = _post(x.reshape(B * T, D), go.reshape(B * T, GLA_W), so.reshape(B * T, SB_W),
                  co.reshape(B * T, CONV_CH), pre_ln, ln_g, ln_b, p["wo"], p["g1"], p["b1"],
                  p["wu"], p["wd"], p["g2"], p["b2"]).reshape(B, T, D)
        ks.append(skf)
        vs.append(svf)
        gs.append(s_fin)
        cs.append(cst)
    if kv_all is not None:
        k_out, v_out = (jnp.transpose(a.reshape(DEPTH, B, SB_HEADS, SB_DH, T), (0, 1, 4, 2, 3)) for a in kv_all)
    else:
        k_out, v_out = (jnp.stack(a).reshape(DEPTH, B, T, SB_HEADS, SB_DH) for a in (ks, vs))
    return x, k_out, v_out, jnp.stack(gs), jnp.stack(cs)


def kernel(x_prompt, x_sample, cache_sb_k, cache_sb_v, state_gla, state_conv, ln_in_g, ln_in_b, w_in,
           w_gate_up, b_gate, gla_norm_g, conv_w, w_out, ln1_g, ln1_b, w_up, w_down, ln2_g, ln2_b):
    layers = [_prep_layer(l, w_in, w_gate_up, b_gate, gla_norm_g, conv_w, w_out, ln1_g, ln1_b, w_up,
                          w_down, ln2_g, ln2_b) for l in range(DEPTH)]
    ln_g, ln_b = ln_in_g.reshape(1, -1), ln_in_b.reshape(1, -1)
    y_p, k_p, v_p, gla_p, conv_p = _trunk(x_prompt, None, None, None, None, ln_g, ln_b, layers)
    y_s, k_s, v_s, gla_s, conv_s = _trunk(x_sample, cache_sb_k, cache_sb_v, state_gla, state_conv,
                                          ln_g, ln_b, layers)
    return (y_p, y_s, k_p, v_p, gla_p, conv_p, k_s, v_s, gla_s, conv_s)
```

```python
import functools

import numpy as np

import jax
import jax.numpy as jnp
from jax import lax
from jax.experimental import pallas as pl
from jax.experimental.pallas import tpu as pltpu

F32 = jnp.float32
BF16 = jnp.bfloat16

D_MODEL = 1024
DEPTH = 2
CHUNK = 64
GLA_HEADS = 4
GLA_DK = 64
GLA_DV = 128
GLA_GATE_RANK = 16
GLA_TAU = 16.0
GLA_W = GLA_HEADS * GLA_DV
GLA_KW = GLA_HEADS * GLA_DK
SB_HEADS = 4
SB_DH = 64
SB_W = SB_HEADS * SB_DH
CONV_CH = 256
CONV_W = 3
MIX_W = GLA_W + SB_W + CONV_CH
D_FF = 4 * D_MODEL
LN_EPS = 1e-5
DEEPNORM_ALPHA = (2 * DEPTH) ** 0.25
LOG2_E = 1.4426950408889634

LANES = 128
SUBLANES = 8
SB_KBLOCK = 128
SB_DEAD_LOG = -104.0
VMEM_LIMIT_BYTES = 56 * 1024 * 1024
DENSE_TILE_ROWS = 1024
SB_TILES_PER_STEP = 16

_C_GQ = 0
_C_GK = _C_GQ + GLA_KW
_C_GV = _C_GK + GLA_KW
_C_GR = _C_GV + GLA_W
_C_SQ = _C_GR + GLA_W
_C_SK = _C_SQ + SB_W
_C_SV = _C_SK + SB_W
_C_CB = _C_SV + SB_W
_C_CC = _C_CB + CONV_CH
_C_CH = _C_CC + CONV_CH
_C_GLOW = _C_CH + CONV_CH
N_IN_PAD = _C_GLOW + LANES
_GLOW_SRC = 2 * GLA_KW + 2 * GLA_W


def _layer_norm(x, g, b):
    mu = jnp.mean(x, axis=-1, keepdims=True)
    xc = x - mu
    var = jnp.mean(xc * xc, axis=-1, keepdims=True)
    return xc * lax.rsqrt(var + LN_EPS) * g + b


def _softplus_neg_abs(z):
    return jnp.log1p(jnp.exp(-jnp.abs(z)))


def _split_bf16(x, n):
    parts = []
    r = x
    for _ in range(n - 1):
        p = r.astype(BF16)
        parts.append(p)
        r = r - p.astype(F32)
    parts.append(r.astype(BF16))
    return parts


GLA_LEVELS = (32, 16, 8, 4, 2, 1)
ATT_W = GLA_HEADS * CHUNK
B_PAD = SUBLANES
(_K_NEG_UP, _K_NEG_LO, _K_PAIR) = (0, len(GLA_LEVELS), 2 * len(GLA_LEVELS))
_K_EYE = 3 * len(GLA_LEVELS)
_K_POS4 = _K_EYE + 1
_K_ODD = _K_POS4 + 4
_K_COUNT = _K_ODD + 1


def _gla_constants():
    assert ATT_W == GLA_KW
    t = np.arange(CHUNK)[:, None]
    s = (np.arange(ATT_W) % CHUNK)[None, :]
    ninf = np.float32(-np.inf)
    out = np.zeros((_K_COUNT, CHUNK, ATT_W), np.float32)
    for li, h in enumerate(GLA_LEVELS):
        upper = np.broadcast_to((t & h) != 0, (CHUNK, ATT_W))
        out[_K_NEG_UP + li] = np.where(upper, 0.0, ninf)
        out[_K_NEG_LO + li] = np.where(upper, ninf, 0.0)
        out[_K_PAIR + li] = (((t ^ s) // (2 * h)) == 0) & upper & ((s & h) == 0)
    out[_K_EYE] = t == s
    for p in range(4):
        out[_K_POS4 + p] = np.broadcast_to((t & 3) == p, (CHUNK, ATT_W))
    out[_K_ODD] = np.broadcast_to((t & 1) == 1, (CHUNK, ATT_W))
    tril = (np.arange(CHUNK)[None, :] <= np.arange(CHUNK)[:, None]).astype(np.float32)
    hk = (np.arange(ATT_W)[:, None] // CHUNK == np.arange(GLA_KW)[None, :] // GLA_DK).astype(np.float32)
    hv = (np.arange(ATT_W)[:, None] // CHUNK == np.arange(GLA_W)[None, :] // GLA_DV).astype(np.float32)
    return (jnp.asarray(out), jnp.asarray(tril, BF16), jnp.asarray(hk, BF16), jnp.asarray(hv, BF16))


def _gla_stages(q2, k2, v2, lg2, gr2, st, emit, cst_ref, tril_ref, hk_ref, hv_ref, gn_ref, b_ref):
    C = CHUNK
    T = q2.shape[0]
    nc = T // C
    nt = (((1,), (1,)), ((), ()))
    tn = (((0,), (0,)), ((), ()))
    bnn = (((2,), (1,)), ((0,), (0,)))
    bnt = (((2,), (2,)), ((0,), (0,)))
    d = {}

    def chunks(a):
        return a.reshape(nc, C, a.shape[-1])

    q, k, v = chunks(q2), chunks(k2), chunks(v2)

    def decay():
        tril = jnp.broadcast_to(tril_ref[...], (nc, C, C))
        b = sum(lax.dot_general(tril, chunks(p), bnn, preferred_element_type=F32)
                for p in _split_bf16(lg2, 3))
        b_ref[pl.ds(B_PAD, T), :] = b.reshape(T, GLA_KW)
        d["b"] = b

    def carried_operands():
        b = d["b"]
        b_last = b[:, C - 1:C, :]
        d["qe"] = (q * jnp.exp2(b)).astype(BF16)
        d["kd"] = (k * jnp.exp2(b_last - b)).astype(BF16)
        d["e_last"] = jnp.exp2(b_last)

    def scaled_pair(li, h):
        b = d["b"]
        if h >= SUBLANES:
            zeros = jnp.zeros((nc, h, GLA_KW), F32)
            qs, ks_ = [], []
            for m in range(C // (2 * h)):
                lo = slice(m * 2 * h, m * 2 * h + h)
                up = slice(m * 2 * h + h, (m + 1) * 2 * h)
                r = b[:, m * 2 * h + h - 1:m * 2 * h + h, :]
                qs += [zeros, q[:, up] * jnp.exp2(b[:, up] - r)]
                ks_ += [k[:, lo] * jnp.exp2(r - b[:, lo]), zeros]
            return jnp.concatenate(qs, axis=1).astype(BF16), jnp.concatenate(ks_, axis=1).astype(BF16)
        if h == 4:
            r = jnp.concatenate([jnp.broadcast_to(b[:, m * 8 + 3:m * 8 + 4, :], (nc, 8, GLA_KW))
                                 for m in range(C // 8)], axis=1)
        else:
            bm1 = chunks(b_ref[pl.ds(B_PAD - 1, T), :])
            if h == 1:
                r = b + cst_ref[_K_ODD] * (bm1 - b)
            else:
                bm2 = chunks(b_ref[pl.ds(B_PAD - 2, T), :])
                bp1 = chunks(b_ref[pl.ds(B_PAD + 1, T), :])
                r = ((cst_ref[_K_POS4 + 0] * bp1 + cst_ref[_K_POS4 + 1] * b)
                     + (cst_ref[_K_POS4 + 2] * bm1 + cst_ref[_K_POS4 + 3] * bm2))
        return ((q * jnp.exp2((b - r) + cst_ref[_K_NEG_UP + li])).astype(BF16),
                (k * jnp.exp2((r - b) + cst_ref[_K_NEG_LO + li])).astype(BF16))

    def diagonal():
        d["att"] = (chunks(jnp.dot((q2 * k2).astype(BF16), hk_ref[...], preferred_element_type=F32))
                    * cst_ref[_K_EYE])

    def level(li, h):
        qt, kt = scaled_pair(li, h)
        kbd = jnp.concatenate([kt] * GLA_HEADS, axis=1) * hk_ref[...]
        a = lax.dot_general(qt, kbd, bnt, preferred_element_type=F32)
        d["att"] = d["att"] + (a if 2 * h == C else a * cst_ref[_K_PAIR + li])

    def within_chunk():
        vbd = jnp.concatenate([v] * GLA_HEADS, axis=1) * hv_ref[...]
        d["o"] = lax.dot_general(d.pop("att").astype(BF16), vbd, bnn, preferred_element_type=F32)

    def recurrence():
        qe, kd, e_last = d.pop("qe"), d.pop("kd"), d.pop("e_last")
        o_inter = []
        for c in range(nc):
            per_head = []
            for h in range(GLA_HEADS):
                ks = slice(h * GLA_DK, (h + 1) * GLA_DK)
                vs = slice(h * GLA_DV, (h + 1) * GLA_DV)
                per_head.append(lax.dot_general(qe[c][:, ks], st[h].astype(BF16), nt,
                                                preferred_element_type=F32))
                st[h] = st[h] * e_last[c][:, ks] + lax.dot_general(v[c][:, vs], kd[c][:, ks], tn,
                                                                   preferred_element_type=F32)
            o_inter.append(jnp.concatenate(per_head, axis=1))
        d["o"] = d["o"].reshape(T, GLA_W) + jnp.concatenate(o_inter, axis=0)

    def gate():
        o = d.pop("o")
        normed = []
        for h in range(GLA_HEADS):
            oh = o[:, h * GLA_DV:(h + 1) * GLA_DV]
            normed.append(oh * lax.rsqrt(jnp.mean(oh * oh, axis=-1, keepdims=True) + LN_EPS))
        emit(jnp.concatenate(normed, axis=1) * gn_ref[...] * (gr2 * jax.nn.sigmoid(gr2)))

    return ([decay, carried_operands, diagonal]
            + [functools.partial(level, li, h) for li, h in enumerate(GLA_LEVELS)]
            + [within_chunk, recurrence, gate])


IN_PROJ_PART_ROWS = 256
_N_IN_PROJ_INPUTS = 14


def _in_proj_parts(tm):
    if tm % IN_PROJ_PART_ROWS:
        return [(0, tm)]
    parts = [(r, IN_PROJ_PART_ROWS) for r in range(0, tm, IN_PROJ_PART_ROWS)]
    return parts


def _in_proj_kernel(pre_ln, tm, kv_slabs, n_aliased, *refs):
    (x_ref, lng_ref, lnb_ref, w_ref, wg_ref, bg_ref, cw_ref, cprev_ref, gn_ref, s0_ref,
     cst_ref, tril_ref, hk_ref, hv_ref) = refs[:_N_IN_PROJ_INPUTS]
    (go_ref, sfin_ref, sq_ref, skf_ref, svf_ref, skb_ref, svb_ref, co_ref, cst_out_ref,
     uext_ref, st_ref, b_ref) = refs[_N_IN_PROJ_INPUTS + n_aliased:]
    i = pl.program_id(1)

    @pl.when(i == 0)
    def _():
        uext_ref[SUBLANES - 2:SUBLANES, :] = cprev_ref[0]
        for h in range(GLA_HEADS):
            st_ref[h] = s0_ref[0, h].T
        b_ref[...] = jnp.zeros(b_ref.shape, F32)

    parts = _in_proj_parts(tm)

    def normed_rows(r0, part):
        x = x_ref[0, r0:r0 + part, :]
        if pre_ln:
            x = _layer_norm(x, lng_ref[...], lnb_ref[...])
        return x.astype(BF16)

    cw = cw_ref[...]
    st = [st_ref[h] for h in range(GLA_HEADS)]
    pending = []

    def tick():
        if pending:
            pending.pop(0)()

    def store_go(rs, out):
        go_ref[0, rs, :] = out.astype(BF16)

    xb = normed_rows(*parts[0])
    for pi, (r0, part) in enumerate(parts):
        rs = slice(r0, r0 + part)

        def proj(a, b, xb=xb):
            return jnp.dot(xb, w_ref[:, a:b], preferred_element_type=F32)

        glow = proj(_C_GLOW, N_IN_PAD)
        u = jnp.dot(glow.astype(BF16), wg_ref[...], preferred_element_type=F32) + bg_ref[...]
        lg = (jnp.minimum(u, 0.0) - _softplus_neg_abs(u)) * (LOG2_E / GLA_TAU)
        tick()

        cb = proj(_C_CB, _C_CC)
        tick()
        uc = proj(_C_CC, _C_CH) * proj(_C_CH, _C_GLOW)
        uext_ref[SUBLANES + r0:SUBLANES + r0 + part, :] = uc
        um1 = uext_ref[SUBLANES - 1 + r0:SUBLANES - 1 + r0 + part, :]
        um2 = uext_ref[SUBLANES - 2 + r0:SUBLANES - 2 + r0 + part, :]
        cy = (cw[2:3, :] * uc + cw[0:1, :] * um2) + cw[1:2, :] * um1
        co_ref[0, rs, :] = (cb * cy).astype(BF16)
        tick()

        gq = proj(_C_GQ, _C_GK) * (GLA_DK ** -0.5)
        tick()
        gk = proj(_C_GK, _C_GV)
        tick()
        gv = proj(_C_GV, _C_GR).astype(BF16)
        tick()
        xb_next = normed_rows(*parts[pi + 1]) if pi + 1 < len(parts) else None
        tick()
        gr = proj(_C_GR, _C_SQ)
        tick()
        sq_ref[0, rs, :] = (proj(_C_SQ, _C_SK) * (SB_DH ** -0.5)).astype(BF16)
        tick()
        for c0, c1, f_ref, h_ref in ((_C_SK, _C_SV, skf_ref, skb_ref), (_C_SV, _C_CB, svf_ref, svb_ref)):
            kv = proj(c0, c1)
            h_ref[0, rs, :] = kv.astype(BF16)
            if kv_slabs == 0:
                f_ref[0, rs, :] = kv
            else:
                kv_t = kv.T
                for slab in range(kv_slabs):
                    f_ref[slab, 0, :, rs] = kv_t
            tick()
        while pending:
            tick()
        pending.extend(_gla_stages(gq, gk, gv, lg, gr, st, functools.partial(store_go, rs), cst_ref, tril_ref,
                                   hk_ref, hv_ref, gn_ref, b_ref.at[pi]))
        xb = xb_next
    while pending:
        tick()
    tail = uc[part - 2:part, :]
    uext_ref[SUBLANES - 2:SUBLANES, :] = tail
    for h in range(GLA_HEADS):
        st_ref[h] = st[h]

    @pl.when(i == pl.num_programs(1) - 1)
    def _():
        cst_out_ref[0] = tail
        for h in range(GLA_HEADS):
            sfin_ref[0, h] = st[h].T


def _in_proj(x, pre_ln, ln_g, ln_b, w, wg, bg, cw, cprev, gn, s0, kv_layer=None, kv_prev=None):
    B, T, D = x.shape
    tm = min(DENSE_TILE_ROWS, T)
    n = T // tm
    parts = _in_proj_parts(tm)
    assert all(p[1] % CHUNK == 0 for p in parts)
    row = lambda width: pl.BlockSpec((1, tm, width), lambda b, i: (b, i, 0))
    const = lambda a: pl.BlockSpec(a.shape, lambda b, i: (0,) * a.ndim, pipeline_mode=pl.Buffered(1))
    state = pl.BlockSpec((1, GLA_HEADS, GLA_DK, GLA_DV), lambda b, i: (b, 0, 0, 0))
    conv_state = pl.BlockSpec((1, CONV_W - 1, CONV_CH), lambda b, i: (b, 0, 0))
    stacked = kv_layer is not None
    assert (kv_prev is None) == (not stacked or kv_layer == 0)
    aliased = [] if kv_prev is None else list(kv_prev)
    kv_slabs = 0 if not stacked else (DEPTH if kv_layer == 0 else 1)
    kv_shape = (DEPTH, B, SB_W, T) if stacked else (B, T, SB_W)
    kv_spec = (pl.BlockSpec((kv_slabs, 1, SB_W, tm), lambda b, i: (kv_layer, b, 0, i)) if stacked
               else row(SB_W))
    consts = _gla_constants()
    inputs = [x, ln_g, ln_b, w, wg, bg, cw, cprev, gn, s0, *consts]
    assert len(inputs) == _N_IN_PROJ_INPUTS
    in_specs = ([row(D), const(ln_g), const(ln_b), const(w), const(wg), const(bg), const(cw), conv_state,
                 const(gn), state] + [const(c) for c in consts]
                + [pl.BlockSpec(memory_space=pl.ANY)] * len(aliased))
    outs = [
        (jax.ShapeDtypeStruct((B, T, GLA_W), BF16), row(GLA_W)),
        (jax.ShapeDtypeStruct((B, GLA_HEADS, GLA_DK, GLA_DV), F32), state),
        (jax.ShapeDtypeStruct((B, T, SB_W), BF16), row(SB_W)),
        (jax.ShapeDtypeStruct(kv_shape, F32), kv_spec),
        (jax.ShapeDtypeStruct(kv_shape, F32), kv_spec),
        (jax.ShapeDtypeStruct((B, T, SB_W), BF16), row(SB_W)),
        (jax.ShapeDtypeStruct((B, T, SB_W), BF16), row(SB_W)),
        (jax.ShapeDtypeStruct((B, T, CONV_CH), BF16), row(CONV_CH)),
        (jax.ShapeDtypeStruct((B, CONV_W - 1, CONV_CH), F32), conv_state),
    ]
    return pl.pallas_call(
        functools.partial(_in_proj_kernel, pre_ln, tm, kv_slabs, len(aliased)),
        grid=(B, n),
        in_specs=in_specs,
        out_specs=[o[1] for o in outs],
        out_shape=[o[0] for o in outs],
        input_output_aliases={_N_IN_PROJ_INPUTS + j: 3 + j for j in range(len(aliased))},
        scratch_shapes=[pltpu.VMEM((SUBLANES + tm, CONV_CH), F32),
                        pltpu.VMEM((GLA_HEADS, GLA_DV, GLA_DK), F32),
                        pltpu.VMEM((len(parts), max(p[1] for p in parts) + 2 * B_PAD, GLA_KW), F32)],
        compiler_params=pltpu.CompilerParams(dimension_semantics=("arbitrary", "arbitrary"),
                                             vmem_limit_bytes=VMEM_LIMIT_BYTES),
        name="in_proj",
    )(*inputs, *aliased)


SB_WINDOW_PAST = 256


def _suffix_weights():
    j = (np.arange(2 * LANES) % LANES)[:, None]
    s = np.arange(2 * LANES)[None, :]
    return jnp.asarray((j > s) | (s >= LANES), BF16)


SB_MASKED_LOGIT = -1e30


def _sb_step(q_tiles, kws, vws, w, prefix, lims, col_minus_row):
    tq, tw = q_tiles[0].shape[0], kws[0].shape[0]
    n_tiles = len(q_tiles)
    nt = (((1,), (1,)), ((), ()))
    heads = [slice(h * SB_DH, (h + 1) * SB_DH) for h in range(SB_HEADS)]
    rows = SB_HEADS * tq
    st = [dict() for _ in range(n_tiles)]

    def logits(u):
        z = jnp.concatenate([lax.dot_general(q_tiles[u][:, hs], kws[u][:, hs], nt,
                                             preferred_element_type=F32) for hs in heads], axis=0)
        if lims is not None:
            z = jnp.where(col_minus_row < lims[u], z, SB_MASKED_LOGIT)
        st[u]["z"] = z

    def logs(u):
        z = st[u].pop("z")
        log_beta = jnp.minimum(z, 0.0) - jnp.log(1.0 + jnp.exp(-jnp.abs(z)))
        st[u]["log_beta"] = log_beta
        st[u]["keep"] = _split_bf16(log_beta - z, 2)

    def suffix(u):
        hi, lo = st[u].pop("keep")
        before = None if prefix is None else prefix[u * rows:(u + 1) * rows]
        after = [None] * (tw // LANES)
        total = None
        for blk in reversed(range(tw // LANES)):
            cs = slice(blk * LANES, (blk + 1) * LANES)
            res = jnp.dot(jnp.concatenate([hi[:, cs], lo[:, cs]], axis=1), w, preferred_element_type=F32)
            later = total if before is None else (before if total is None else before + total)
            after[blk] = res[:, :LANES] if later is None else res[:, :LANES] + later
            total = res[:, LANES:] if total is None else total + res[:, LANES:]
        st[u]["after"] = after
        st[u]["total"] = total

    def weights(u):
        after = st[u].pop("after")
        after = after[0] if len(after) == 1 else jnp.concatenate(after, axis=1)
        st[u]["a"] = jnp.exp(st[u].pop("log_beta") + after).astype(BF16)

    def values(u):
        a = st[u].pop("a")
        st[u]["out"] = [jnp.dot(a[h * tq:(h + 1) * tq], vws[u][:, hs], preferred_element_type=F32)
                        for h, hs in enumerate(heads)]

    stages = (logits, logs, suffix, weights, values)
    for step in range(n_tiles + len(stages) - 1):
        for depth in reversed(range(len(stages))):
            if 0 <= step - depth < n_tiles:
                stages[depth](step - depth)
    totals = [s["total"] for s in st]
    return [s["out"] for s in st], totals[0] if n_tiles == 1 else jnp.concatenate(totals, axis=0)


def _sb_kernel(tq, nsub, tw, wp, n_past, layer, past_is_self, *refs):
    if past_is_self:
        q_ref, kp_ref, vp_ref, w_ref, cmr_ref, o_ref = refs
        kp_ref = kp_ref.at[0, 0]
        vp_ref = vp_ref.at[0, 0]
    else:
        (q_ref, kd_ref, vd_ref, kwin_ref, vwin_ref, kc_hbm, vc_hbm, w_ref, cmr_ref, o_ref,
         kbuf_ref, vbuf_ref, sem) = refs
    i = pl.program_id(1)
    q_tiles = [q_ref[0, u * tq:(u + 1) * tq, :] for u in range(nsub)]

    if past_is_self:
        q0s = [(i * nsub + u) * tq for u in range(nsub)]
        rests = [pl.multiple_of(jnp.maximum(q0 - wp, 0), SB_KBLOCK) for q0 in q0s]
        kws = [kp_ref[pl.ds(r, tw), :] for r in rests]
        vws = [vp_ref[pl.ds(r, tw), :] for r in rests]
        lims = [q0 - r for q0, r in zip(q0s, rests)]
    else:
        rests = [n_past - wp]
        pad = jnp.zeros((tw - wp - tq, SB_W), BF16)
        kws = [jnp.concatenate([kwin_ref[0, 0].T.astype(BF16), kd_ref[0], pad], axis=0)]
        vws = [jnp.concatenate([vwin_ref[0, 0].T.astype(BF16), vd_ref[0], pad], axis=0)]
        lims = [wp]

    def past_block(start):
        if past_is_self:
            return kp_ref[pl.ds(start, SB_KBLOCK), :], vp_ref[pl.ds(start, SB_KBLOCK), :]
        b = pl.program_id(0)
        copies = [pltpu.make_async_copy(src.at[layer, b, :, pl.ds(start, SB_KBLOCK)], dst, sem.at[n])
                  for n, (src, dst) in enumerate(((kc_hbm, kbuf_ref), (vc_hbm, vbuf_ref)))]
        for cp in copies:
            cp.start()
        for cp in copies:
            cp.wait()
        return kbuf_ref[...].T.astype(BF16), vbuf_ref[...].T.astype(BF16)
    outs, prefs = _sb_step(q_tiles, kws, vws, w_ref[...], None, lims, cmr_ref[...])

    def store(u, heads_out):
        o_ref[0, u * tq:(u + 1) * tq, :] = jnp.concatenate(heads_out, axis=1).astype(BF16)

    rows = SB_HEADS * tq
    nblks = [r // SB_KBLOCK if isinstance(r, int) else lax.shift_right_logical(r, 7) for r in rests]
    alive = []
    for u in range(nsub):
        store(u, outs[u])
        alive.append(jnp.logical_and(nblks[u] > 0, jnp.max(prefs[u * rows:(u + 1) * rows]) > SB_DEAD_LOG))

    @pl.when(functools.reduce(jnp.logical_or, alive))
    def _():
        for u in range(nsub):
            def cond(carry, nblk=nblks[u]):
                return jnp.logical_and(carry[0] < nblk, carry[1])

            def body(carry, rest=rests[u], u=u):
                j, pref = carry[0], carry[2]
                start = pl.multiple_of(rest - (j + 1) * SB_KBLOCK, SB_KBLOCK)
                kb, vb = past_block(start)
                o_new, p_new = _sb_step([q_tiles[u]], [kb], [vb], w_ref[...], pref, None, None)
                pref = pref + p_new
                return (j + 1, jnp.max(pref) > SB_DEAD_LOG, pref,
                        *[o + n for o, n in zip(carry[3:], o_new[0])])

            final = lax.while_loop(cond, body, (jnp.int32(0), alive[u], prefs[u * rows:(u + 1) * rows],
                                                *outs[u]))
            store(u, final[3:])


def _sb(sq, k_new, v_new, k_past, v_past, layer, past_is_self):
    B, T, _ = sq.shape
    P = k_past.shape[2] if past_is_self else k_past.shape[3]
    tq = min(SB_KBLOCK, T)
    nsub = SB_TILES_PER_STEP if past_is_self and T % (SB_TILES_PER_STEP * tq) == 0 else 1
    n = T // (nsub * tq)
    wp = min(SB_WINDOW_PAST, P - tq if past_is_self else P)
    tw = -(-(wp + tq) // LANES) * LANES
    assert T % tq == 0 and tq & (tq - 1) == 0 and P % SB_KBLOCK == 0 and wp % SB_KBLOCK == 0
    assert T == tq or tq == SB_KBLOCK
    assert not past_is_self or tw == wp + tq
    row = pl.BlockSpec((1, nsub * tq, SB_W), lambda b, i: (b, i, 0))
    const = lambda a: pl.BlockSpec(a.shape, lambda b, i: (0, 0), pipeline_mode=pl.Buffered(1))
    w = _suffix_weights()
    cmr = jnp.asarray(np.arange(tw)[None, :] - (np.arange(SB_HEADS * tq) % tq)[:, None], jnp.int32)
    if past_is_self:
        full = pl.BlockSpec((1, 1, P, SB_W), lambda b, i: (0, b, 0, 0))
        args = [sq, k_past, v_past, w, cmr]
        specs = [row, full, full, const(w), const(cmr)]
        scratch = []
    else:
        assert P % wp == 0
        win = pl.BlockSpec((1, 1, SB_W, wp), lambda b, i: (layer, b, 0, P // wp - 1))
        hbm = pl.BlockSpec(memory_space=pl.ANY)
        args = [sq, k_new, v_new, k_past, v_past, k_past, v_past, w, cmr]
        specs = [row, row, row, win, win, hbm, hbm, const(w), const(cmr)]
        scratch = [pltpu.VMEM((SB_W, SB_KBLOCK), F32), pltpu.VMEM((SB_W, SB_KBLOCK), F32),
                   pltpu.SemaphoreType.DMA((2,))]
    return pl.pallas_call(
        functools.partial(_sb_kernel, tq, nsub, tw, wp, P, layer, past_is_self),
        grid=(B, n),
        in_specs=specs,
        out_specs=row,
        out_shape=jax.ShapeDtypeStruct((B, T, SB_W), BF16),
        scratch_shapes=scratch,
        compiler_params=pltpu.CompilerParams(dimension_semantics=("arbitrary", "arbitrary"),
                                             vmem_limit_bytes=VMEM_LIMIT_BYTES),
        name="sb",
    )(*args)


FF_CHUNK = 1024
POST_PART_ROWS = 256


def _post_kernel(pre_ln, x_ref, go_ref, so_ref, co_ref, lng_ref, lnb_ref, wo_ref, g1_ref, b1_ref,
                 wu_ref, wd_ref, g2_ref, b2_ref, y_ref):
    tm = x_ref.shape[0]
    n_ff = D_FF // FF_CHUNK
    if tm % POST_PART_ROWS == 0 and POST_PART_ROWS % (n_ff * SUBLANES) == 0:
        halves = [(r, POST_PART_ROWS) for r in range(0, tm, POST_PART_ROWS)]
    else:
        halves = [(0, tm)]

    def mix_rows(r0, n):
        rs = slice(r0, r0 + n)
        return (jnp.dot(go_ref[rs, :], wo_ref[0:GLA_W, :], preferred_element_type=F32)
                + jnp.dot(so_ref[rs, :], wo_ref[GLA_W:GLA_W + SB_W, :], preferred_element_type=F32)
                + jnp.dot(co_ref[rs, :], wo_ref[GLA_W + SB_W:MIX_W, :], preferred_element_type=F32))

    def ln1_rows(mix, r0, n, off):
        x = x_ref[r0 + off:r0 + off + n, :]
        if pre_ln:
            x = _layer_norm(x, lng_ref[...], lnb_ref[...])
        return _layer_norm(DEEPNORM_ALPHA * x + mix[off:off + n], g1_ref[...], b1_ref[...])

    def ffn_chunk(x1b, c):
        c0 = c * FF_CHUNK
        h = jnp.dot(x1b, wu_ref[:, c0:c0 + FF_CHUNK], preferred_element_type=F32)
        h = jnp.square(jnp.maximum(h, 0.0)).astype(BF16)
        return jnp.dot(h, wd_ref[c0:c0 + FF_CHUNK, :], preferred_element_type=F32)

    def ln2_rows(x1, ff, r0, n, off):
        y_ref[r0 + off:r0 + off + n, :] = _layer_norm(DEEPNORM_ALPHA * x1[off:off + n] + ff[off:off + n],
                                                     g2_ref[...], b2_ref[...])

    r0, n = halves[0]
    x1 = ln1_rows(mix_rows(r0, n), r0, n, 0)
    prev = None
    for hi in range(len(halves)):
        r0, n = halves[hi]
        nxt = halves[hi + 1] if hi + 1 < len(halves) else None
        nxt_mix = mix_rows(*nxt) if nxt else None
        x1b = x1.astype(BF16)
        g = n // n_ff
        ff, nxt_x1 = None, []
        for c in range(n_ff):
            part = ffn_chunk(x1b, c)
            ff = part if ff is None else ff + part
            if prev is not None:
                ln2_rows(prev[0], prev[1], prev[2], g, c * g)
            if nxt:
                nxt_x1.append(ln1_rows(nxt_mix, nxt[0], g, c * g))
        prev = (x1, ff, r0, n)
        if nxt:
            x1 = jnp.concatenate(nxt_x1, axis=0)
    ln2_rows(prev[0], prev[1], prev[2], prev[3], 0)


def _post(x, go, so, co, pre_ln, ln_g, ln_b, wo, g1, b1, wu, wd, g2, b2):
    N, D = x.shape
    tm = min(DENSE_TILE_ROWS, N)
    row = lambda width: pl.BlockSpec((tm, width), lambda i: (i, 0))
    const = lambda a: pl.BlockSpec(a.shape, lambda i: (0, 0), pipeline_mode=pl.Buffered(1))
    return pl.pallas_call(
        functools.partial(_post_kernel, pre_ln),
        grid=(N // tm,),
        in_specs=[row(D), row(GLA_W), row(SB_W), row(CONV_CH), const(ln_g), const(ln_b), const(wo),
                  const(g1), const(b1), const(wu), const(wd), const(g2), const(b2)],
        out_specs=row(D),
        out_shape=jax.ShapeDtypeStruct((N, D), F32),
        compiler_params=pltpu.CompilerParams(dimension_semantics=("arbitrary",),
                                             vmem_limit_bytes=VMEM_LIMIT_BYTES),
        name="post",
    )(x, go, so, co, ln_g, ln_b, wo, g1, b1, wu, wd, g2, b2)


def _prep_layer(l, w_in, w_gate_up, b_gate, gla_norm_g, conv_w, w_out, ln1_g, ln1_b, w_up, w_down,
                ln2_g, ln2_b):
    wl = w_in[l]
    w = jnp.concatenate([wl[:, :_GLOW_SRC], wl[:, _GLOW_SRC + GLA_GATE_RANK:],
                         wl[:, _GLOW_SRC:_GLOW_SRC + GLA_GATE_RANK],
                         jnp.zeros((D_MODEL, LANES - GLA_GATE_RANK), F32)], axis=1).astype(BF16)
    wg = jnp.concatenate([w_gate_up[l], jnp.zeros((LANES - GLA_GATE_RANK, GLA_KW), F32)], axis=0).astype(BF16)
    r2 = lambda a: a.reshape(1, -1)
    return dict(w=w, wg=wg, bg=r2(b_gate[l]), gn=r2(gla_norm_g[l]), cw=conv_w[l],
                wo=w_out[l].astype(BF16), g1=r2(ln1_g[l]), b1=r2(ln1_b[l]),
                wu=w_up[l].astype(BF16), wd=w_down[l].astype(BF16), g2=r2(ln2_g[l]), b2=r2(ln2_b[l]))


def _trunk(x, c_k, c_v, s_gla, s_conv, ln_g, ln_b, layers):
    B, T, D = x.shape
    ks, vs, gs, cs = [], [], [], []
    kv_all = None
    for l, p in enumerate(layers):
        pre_ln = l == 0
        if c_k is None:
            s0 = jnp.zeros((B, GLA_HEADS, GLA_DK, GLA_DV), F32)
            cprev = jnp.zeros((B, CONV_W - 1, CONV_CH), F32)
            kv_args = dict(kv_layer=l, kv_prev=kv_all)
        else:
            s0, cprev = s_gla[l], s_conv[l]
            kv_args = {}
        go, s_fin, sq, skf, svf, skb, svb, co, cst = _in_proj(
            x, pre_ln, ln_g, ln_b, p["w"], p["wg"], p["bg"], p["cw"], cprev, p["gn"], s0, **kv_args)
        if c_k is None:
            kv_all = (skf, svf)
            so = _sb(sq, skb, svb, skb[None], svb[None], 0, True)
        else:
            key_minor = lambda c: jnp.transpose(c, (0, 1, 3, 4, 2)).reshape(DEPTH, B, SB_W, -1)
            so = _sb(sq, skb, svb, key_minor(c_k), key_minor(c_v), l, False)
        x = _post(x.reshape(B * T, D), go.reshape(B * T, GLA_W), so.reshape(B * T, SB_W),
                  co.reshape(B * T, CONV_CH), pre_ln, ln_g, ln_b, p["wo"], p["g1"], p["b1"],
                  p["wu"], p["wd"], p["g2"], p["b2"]).reshape(B, T, D)
        ks.append(skf)
        vs.append(svf)
        gs.append(s_fin)
        cs.append(cst)
    if kv_all is not None:
        k_out, v_out = (jnp.transpose(a.reshape(DEPTH, B, SB_HEADS, SB_DH, T), (0, 1, 4, 2, 3)) for a in kv_all)
    else:
        k_out, v_out = (jnp.stack(a).reshape(DEPTH, B, T, SB_HEADS, SB_DH) for a in (ks, vs))
    return x, k_out, v_out, jnp.stack(gs), jnp.stack(cs)


def kernel(x_prompt, x_sample, cache_sb_k, cache_sb_v, state_gla, state_conv, ln_in_g, ln_in_b, w_in,
           w_gate_up, b_gate, gla_norm_g, conv_w, w_out, ln1_g, ln1_b, w_up, w_down, ln2_g, ln2_b):
    layers = [_prep_layer(l, w_in, w_gate_up, b_gate, gla_norm_g, conv_w, w_out, ln1_g, ln1_b, w_up,
                          w_down, ln2_g, ln2_b) for l in range(DEPTH)]
    ln_g, ln_b = ln_in_g.reshape(1, -1), ln_in_b.reshape(1, -1)
    y_p, k_p, v_p, gla_p, conv_p = _trunk(x_prompt, None, None, None, None, ln_g, ln_b, layers)
    y_s, k_s, v_s, gla_s, conv_s = _trunk(x_sample, cache_sb_k, cache_sb_v, state_gla, state_conv,
                                          ln_g, ln_b, layers)
    return (y_p, y_s, k_p, v_p, gla_p, conv_p, k_s, v_s, gla_s, conv_s)
```
